```python
import math
import jax, jax.numpy as jnp
from jax import lax
import numpy as np

D_MODEL = 4096
BATCH = 1
SEQ = 8192
DEPTH = 1

MEM_LEN = 256
XA_HEADS = 4
XA_HEAD_DIM = 128
XA_WIDTH = XA_HEADS * XA_HEAD_DIM
SB_HEADS = 16
SB_HEAD_DIM = 128
SB_WIDTH = SB_HEADS * SB_HEAD_DIM
Q_BLOCK = 128
POOL_WINDOWS = (2, 4, 8, 16)
POOL_GROUPS = len(POOL_WINDOWS)
POOL_WIDTH = D_MODEL // 2
POOL_GROUP_DIM = POOL_WIDTH // POOL_GROUPS
IN_WIDTH = 3 * SB_WIDTH + POOL_WIDTH + 2 * D_MODEL
PEER_KEYS = 128
PEER_EXPERTS = PEER_KEYS * PEER_KEYS
PEER_HEADS = 8
PEER_QUERY_DIM = 256
PEER_HALF = PEER_QUERY_DIM // 2
PEER_TOPK = 16
PEER_CHUNK = 64
EPS = 1e-6

kernel_name = "hybrid_sba_pool_peer_block"


def rmsnorm(x, g):
    xf = x.astype(jnp.float32)
    y = xf * lax.rsqrt(jnp.mean(xf * xf, axis=-1, keepdims=True) + EPS)
    return (y * g.astype(jnp.float32)).astype(x.dtype)


def stick_breaking_attention(q, k, v):
    B, H, S, Dh = q.shape
    scale = Dh ** -0.5
    outs = []
    for blk in range(S // Q_BLOCK):
        t0 = blk * Q_BLOCK
        L = t0 + Q_BLOCK
        z = jnp.einsum('bhqd,bhkd->bhqk', q[:, :, t0:L], k[:, :, :L]).astype(jnp.float32) * scale
        t_idx = t0 + jnp.arange(Q_BLOCK)
        s_idx = jnp.arange(L)
        causal = s_idx[None, :] < t_idx[:, None]
        sp = jnp.where(causal, jax.nn.softplus(z), 0.0)
        between = lax.cumsum(sp, axis=3, reverse=True) - sp
        a = jnp.where(causal, jnp.exp(jax.nn.log_sigmoid(z) - between), 0.0)
        outs.append(jnp.einsum('bhqk,bhkd->bhqd', a.astype(v.dtype), v[:, :, :L]))
    return jnp.concatenate(outs, axis=2)


def multiscale_pool(u, w_groups, pool_scale):
    B, S, _ = u.shape
    ug = u.reshape(B, S, POOL_GROUPS, POOL_GROUP_DIM).astype(jnp.float32)
    cs = jnp.cumsum(ug, axis=1)
    pos = jnp.arange(S)
    outs = []
    for g, w in enumerate(POOL_WINDOWS):
        c = cs[:, :, g]
        lagged = jnp.pad(c, ((0, 0), (w, 0), (0, 0)))[:, :S]
        count = jnp.minimum(pos + 1, w).astype(jnp.float32)[None, :, None]
        outs.append((c - lagged) / count - ug[:, :, g])
    pooled = jnp.stack(outs, axis=2).astype(u.dtype)
    mixed = jnp.einsum('bsgc,gcd->bsgd', pooled, w_groups)
    return mixed.reshape(B, S, POOL_WIDTH) * pool_scale


def memory_cross_attention(h, mem_n, w_q, w_kv, w_o):
    B, S, _ = h.shape
    M = mem_n.shape[1]
    q = (h @ w_q).reshape(B, S, XA_HEADS, XA_HEAD_DIM)
    kv = (mem_n @ w_kv).reshape(B, M, 2, XA_HEADS, XA_HEAD_DIM)
    k, v = kv[:, :, 0], kv[:, :, 1]
    s = jnp.einsum('bqhd,bkhd->bhqk', q, k).astype(jnp.float32) * (XA_HEAD_DIM ** -0.5)
    p = jax.nn.softmax(s, axis=-1).astype(h.dtype)
    o = jnp.einsum('bhqk,bkhd->bqhd', p, v).reshape(B, S, XA_WIDTH)
    return o @ w_o


def peer_ffn(h, w_query, sub_keys, expert_down, expert_up):
    B, S, D = h.shape
    N = B * S
    tokens = h.reshape(N, D)
    q = (tokens @ w_query).reshape(N, PEER_HEADS, 2, PEER_HALF)
    scores = jnp.einsum('nhpc,hpkc->nhpk', q, sub_keys).astype(jnp.float32)
    s1, i1 = lax.top_k(scores[:, :, 0], PEER_TOPK)
    s2, i2 = lax.top_k(scores[:, :, 1], PEER_TOPK)
    cand = (s1[..., :, None] + s2[..., None, :]).reshape(N, PEER_HEADS, PEER_TOPK * PEER_TOPK)
    cand_id = (i1[..., :, None] * PEER_KEYS + i2[..., None, :]).reshape(N, PEER_HEADS, PEER_TOPK * PEER_TOPK)
    best, sel = lax.top_k(cand, PEER_TOPK)
    expert_id = jnp.take_along_axis(cand_id, sel, axis=-1)
    gate = jax.nn.softmax(best, axis=-1)
    n_chunks = N // PEER_CHUNK

    def chunk_fn(args):
        xc, idc, gc = args
        u = jnp.take(expert_down, idc, axis=0)
        act = jax.nn.gelu(jnp.einsum('chkd,cd->chk', u, xc).astype(jnp.float32), approximate=False)
        w = (gc * act).astype(xc.dtype)
        v = jnp.take(expert_up, idc, axis=0)
        return jnp.einsum('chk,chkd->cd', w, v)

    out = lax.map(chunk_fn, (tokens.reshape(n_chunks, PEER_CHUNK, D),
                             expert_id.reshape(n_chunks, PEER_CHUNK, PEER_HEADS, PEER_TOPK),
                             gate.reshape(n_chunks, PEER_CHUNK, PEER_HEADS, PEER_TOPK)))
    return out.reshape(B, S, D)


def setup_inputs(seed: int = 0) -> dict:
    key = jax.random.key(seed)
    ks = jax.random.split(key, 24)
    f32 = jnp.float32
    nrm = lambda k, shape, s: jax.random.normal(k, shape, f32) * s
    gain = lambda k, shape: 1.0 + 0.05 * jax.random.normal(k, shape, f32)
    L = DEPTH
    return {
        "x": nrm(ks[0], (BATCH, SEQ, D_MODEL), 1.0),
        "mem": nrm(ks[1], (BATCH, MEM_LEN, D_MODEL), 1.0),
        "norm_mix": gain(ks[2], (L, D_MODEL)),
        "norm_mem_q": gain(ks[3], (L, D_MODEL)),
        "norm_mem_kv": gain(ks[4], (L, D_MODEL)),
        "norm_ffn": gain(ks[5], (L, D_MODEL)),
        "norm_final": gain(ks[6], (D_MODEL,)),
        "w_in": nrm(ks[7], (L, D_MODEL, IN_WIDTH), D_MODEL ** -0.5),
        "pool_group_w": nrm(ks[8], (L, POOL_GROUPS, POOL_GROUP_DIM, POOL_GROUP_DIM), POOL_GROUP_DIM ** -0.5),
        "pool_scale": 0.5 + 0.1 * jax.random.normal(ks[9], (L, POOL_WIDTH), f32),
        "w_branch_sb": nrm(ks[10], (L, SB_WIDTH, D_MODEL), SB_WIDTH ** -0.5),
        "w_branch_pool": nrm(ks[11], (L, POOL_WIDTH, D_MODEL), POOL_WIDTH ** -0.5),
        "w_out": nrm(ks[12], (L, D_MODEL, D_MODEL), D_MODEL ** -0.5),
        "xa_w_q": nrm(ks[13], (L, D_MODEL, XA_WIDTH), D_MODEL ** -0.5),
        "xa_w_kv": nrm(ks[14], (L, D_MODEL, 2 * XA_WIDTH), D_MODEL ** -0.5),
        "xa_w_o": nrm(ks[15], (L, XA_WIDTH, D_MODEL), XA_WIDTH ** -0.5),
        "peer_w_query": nrm(ks[16], (L, D_MODEL, PEER_HEADS * PEER_QUERY_DIM), D_MODEL ** -0.5),
        "peer_sub_keys": nrm(ks[17], (L, PEER_HEADS, 2, PEER_KEYS, PEER_HALF), PEER_HALF ** -0.5),
        "peer_down": nrm(ks[18], (L, PEER_EXPERTS, D_MODEL), D_MODEL ** -0.5),
        "peer_up": nrm(ks[19], (L, PEER_EXPERTS, D_MODEL), PEER_HEADS ** -0.5),
    }


def reference(x, mem, norm_mix, norm_mem_q, norm_mem_kv, norm_ffn, norm_final, w_in, pool_group_w,
              pool_scale, w_branch_sb, w_branch_pool, w_out, xa_w_q, xa_w_kv, xa_w_o,
              peer_w_query, peer_sub_keys, peer_down, peer_up):
    B, S, _ = x.shape
    o_k = SB_WIDTH
    o_v = 2 * SB_WIDTH
    o_pool = 3 * SB_WIDTH
    o_gsb = o_pool + POOL_WIDTH
    o_gpool = o_gsb + D_MODEL
    for l in range(DEPTH):
        h = rmsnorm(x, norm_mix[l])
        proj = h @ w_in[l]
        to_heads = lambda t: t.reshape(B, S, SB_HEADS, SB_HEAD_DIM).transpose(0, 2, 1, 3)
        q = to_heads(proj[..., :o_k])
        k = to_heads(proj[..., o_k:o_v])
        v = to_heads(proj[..., o_v:o_pool])
        u_pool = proj[..., o_pool:o_gsb]
        g_sb = jax.nn.sigmoid(proj[..., o_gsb:o_gpool])
        g_pool = jax.nn.sigmoid(proj[..., o_gpool:])
        attn = stick_breaking_attention(q, k, v).transpose(0, 2, 1, 3).reshape(B, S, SB_WIDTH)
        y_sb = attn @ w_branch_sb[l]
        y_pool = multiscale_pool(u_pool, pool_group_w[l], pool_scale[l]) @ w_branch_pool[l]
        x = x + (g_sb * y_sb + g_pool * y_pool) @ w_out[l]
        mem_n = rmsnorm(mem, norm_mem_kv[l])
        x = x + memory_cross_attention(rmsnorm(x, norm_mem_q[l]), mem_n, xa_w_q[l], xa_w_kv[l], xa_w_o[l])
        x = x + peer_ffn(rmsnorm(x, norm_ffn[l]), peer_w_query[l], peer_sub_keys[l], peer_down[l], peer_up[l])
    return rmsnorm(x, norm_final)
```

```python
import functools
import math

import jax
import jax.numpy as jnp
from jax import lax
from jax.experimental import pallas as pl
from jax.experimental.pallas import tpu as pltpu

F32 = jnp.float32
BF16 = jnp.bfloat16

HEAD_DIM = 128
POOL_WINDOWS = (2, 4, 8, 16)
PEER_TOPK = 16
EPS = 1e-6

V7X_VMEM_BYTES = 64 * 2**20
V7X_VMEM_BUDGET = 56 * 2**20
LANES = 128
SUBLANES = 8

SB_DECAY_STOP = 120.0
RANK_NONE = 1.0e9


def _vmem_limit(pipelined_bytes, scratch_bytes=0):
    need = 2 * pipelined_bytes + scratch_bytes + 12 * 2**20
    return int(min(V7X_VMEM_BUDGET, max(need, 16 * 2**20)))


def _nbytes(shape, dtype):
    return math.prod(shape) * jnp.dtype(dtype).itemsize


def _params(semantics, pipelined_bytes, scratch_bytes=0):
    return pltpu.CompilerParams(
        dimension_semantics=semantics,
        vmem_limit_bytes=_vmem_limit(pipelined_bytes, scratch_bytes))


def _tile(n, pref):
    t = min(n, pref)
    assert n % t == 0, (n, pref)
    return t


def _rmsnorm_rows(x, g):
    return x * lax.rsqrt(jnp.mean(x * x, axis=-1, keepdims=True) + EPS) * g


def _rmsnorm_kernel(x_ref, g_ref, o_ref, *, transpose):
    y = _rmsnorm_rows(x_ref[...], g_ref[...])
    if transpose:
        y = y.T
    o_ref[...] = y.astype(o_ref.dtype)


def _rmsnorm(x, g, *, out_dtype, transpose=False):
    m, d = x.shape
    bm = _tile(m, 256)
    if transpose:
        out_shape = jax.ShapeDtypeStruct((d, m), out_dtype)
        out_spec = pl.BlockSpec((d, bm), lambda i: (0, i))
    else:
        out_shape = jax.ShapeDtypeStruct((m, d), out_dtype)
        out_spec = pl.BlockSpec((bm, d), lambda i: (i, 0))
    nbytes = _nbytes((bm, d), F32) + _nbytes((bm, d), out_dtype)
    return pl.pallas_call(
        functools.partial(_rmsnorm_kernel, transpose=transpose),
        grid=(m // bm,),
        in_specs=[pl.BlockSpec((bm, d), lambda i: (i, 0)),
                  pl.BlockSpec((1, d), lambda i: (0, 0))],
        out_specs=out_spec,
        out_shape=out_shape,
        compiler_params=_params(("parallel",), nbytes, 2 * _nbytes((bm, d), F32)),
        name="rmsnorm",
    )(x, g.reshape(1, d))


def _mm_kernel(a_ref, w_ref, *rest, epilogue):
    o_ref = rest[-1]
    acc = jnp.dot(a_ref[...], w_ref[...], preferred_element_type=F32)
    if epilogue == "sigmoid":
        acc = jax.nn.sigmoid(acc)
    elif epilogue == "residual":
        acc = rest[0][...] + acc
    o_ref[...] = acc.astype(o_ref.dtype)


def _matmul(a, w, *, out_dtype, epilogue="none", residual=None, bm=1024, bn=1024):
    m, k = a.shape
    k2, n = w.shape
    assert k == k2
    bm = _tile(m, bm)
    bn = _tile(n, bn)
    in_specs = [pl.BlockSpec((bm, k), lambda i, j: (i, 0)),
                pl.BlockSpec((k, bn), lambda i, j: (0, j))]
    args = [a, w]
    nbytes = _nbytes((bm, k), a.dtype) + _nbytes((k, bn), w.dtype) + _nbytes((bm, bn), out_dtype)
    if epilogue == "residual":
        in_specs.append(pl.BlockSpec((bm, bn), lambda i, j: (i, j)))
        args.append(residual)
        nbytes += _nbytes((bm, bn), residual.dtype)
    return pl.pallas_call(
        functools.partial(_mm_kernel, epilogue=epilogue),
        grid=(m // bm, n // bn),
        in_specs=in_specs,
        out_specs=pl.BlockSpec((bm, bn), lambda i, j: (i, j)),
        out_shape=jax.ShapeDtypeStruct((m, n), out_dtype),
        compiler_params=_params(("parallel", "parallel"), nbytes, _nbytes((bm, bn), F32)),
        name="matmul_" + epilogue,
    )(*args)


def _mix_kernel(a1_ref, a2_ref, w1_ref, w2_ref, g1_ref, g2_ref, o_ref):
    y1 = jnp.dot(a1_ref[...], w1_ref[...], preferred_element_type=F32)
    y2 = jnp.dot(a2_ref[...], w2_ref[...], preferred_element_type=F32)
    o_ref[...] = (g1_ref[...] * y1 + g2_ref[...] * y2).astype(o_ref.dtype)


def _gated_mix(attn, pooled, w_sb, w_pool, gates, *, d_model):
    s, k1 = attn.shape
    k2 = pooled.shape[1]
    bm = _tile(s, 1024)
    bn = _tile(d_model, 512)
    nj = d_model // bn
    nbytes = (_nbytes((bm, k1), BF16) + _nbytes((bm, k2), BF16) + _nbytes((k1, bn), BF16)
              + _nbytes((k2, bn), BF16) + 2 * _nbytes((bm, bn), F32) + _nbytes((bm, bn), BF16))
    return pl.pallas_call(
        _mix_kernel,
        grid=(s // bm, nj),
        in_specs=[pl.BlockSpec((bm, k1), lambda i, j: (i, 0)),
                  pl.BlockSpec((bm, k2), lambda i, j: (i, 0)),
                  pl.BlockSpec((k1, bn), lambda i, j: (0, j)),
                  pl.BlockSpec((k2, bn), lambda i, j: (0, j)),
                  pl.BlockSpec((bm, bn), lambda i, j: (i, j)),
                  pl.BlockSpec((bm, bn), lambda i, j, nj=nj: (i, j + nj))],
        out_specs=pl.BlockSpec((bm, bn), lambda i, j: (i, j)),
        out_shape=jax.ShapeDtypeStruct((s, d_model), BF16),
        compiler_params=_params(("parallel", "parallel"), nbytes, 3 * _nbytes((bm, bn), F32)),
        name="gated_mix",
    )(attn, pooled, w_sb, w_pool, gates, gates)


def _softplus(z):
    return jnp.maximum(z, 0.0) + jnp.log1p(jnp.exp(-jnp.abs(z)))


def _sb_attn_kernel(q_ref, k_ref, v_ref, o_ref, acc_ref, carry_ref, *, tq, scale):
    qi = pl.program_id(1)
    q = q_ref[...]
    acc_ref[...] = jnp.zeros_like(acc_ref)
    carry_ref[...] = jnp.zeros_like(carry_ref)
    row = lax.broadcasted_iota(jnp.int32, (tq, tq), 0)
    col = lax.broadcasted_iota(jnp.int32, (tq, tq), 1)
    later = (lax.broadcasted_iota(jnp.int32, (tq, 2 * tq), 0)
             > lax.broadcasted_iota(jnp.int32, (tq, 2 * tq), 1) % tq)
    whole = lax.broadcasted_iota(jnp.int32, (tq, 2 * tq), 1) >= tq
    suffix_mat = jnp.where(later | whole, 1.0, 0.0).astype(BF16)

    def cond(state):
        kb, min_carry = state
        return jnp.logical_and(kb >= 0, min_carry < SB_DECAY_STOP)

    def body(state):
        kb, _ = state
        ks = pl.multiple_of(kb * tq, tq)
        kblk = k_ref[pl.ds(ks, tq), :]
        vblk = v_ref[pl.ds(ks, tq), :]
        z = lax.dot_general(q, kblk, (((1,), (1,)), ((), ())),
                            preferred_element_type=F32) * scale
        spz = _softplus(z)
        causal = col < row + jnp.where(kb == qi, 0, tq)
        sp = jnp.where(causal, spz, 0.0)
        sp_hi = sp.astype(BF16)
        sp_lo = (sp - sp_hi.astype(F32)).astype(BF16)
        sums = (jnp.dot(sp_hi, suffix_mat, preferred_element_type=F32)
                + jnp.dot(sp_lo, suffix_mat, preferred_element_type=F32))
        carry = carry_ref[...]
        between = sums[:, :tq] + carry
        a = jnp.where(causal, jnp.exp(z - spz - between), 0.0)
        acc_ref[...] += jnp.dot(a.astype(BF16), vblk, preferred_element_type=F32)
        new_carry = carry + sums[:, tq:]
        carry_ref[...] = new_carry
        return kb - 1, jnp.min(new_carry)

    lax.while_loop(cond, body, (qi, jnp.float32(0.0)))
    o_ref[...] = acc_ref[...].astype(o_ref.dtype)


def _sb_attention(qkv, *, n_heads):
    s = qkv.shape[0]
    tq = _tile(s, 256)
    nbytes = (2 * _nbytes((tq, HEAD_DIM), BF16) + 2 * _nbytes((s, HEAD_DIM), BF16))
    scratch = _nbytes((tq, HEAD_DIM), F32) + _nbytes((tq, tq), F32)
    return pl.pallas_call(
        functools.partial(_sb_attn_kernel, tq=tq, scale=HEAD_DIM ** -0.5),
        grid=(n_heads, s // tq),
        in_specs=[pl.BlockSpec((tq, HEAD_DIM), lambda h, i: (i, h)),
                  pl.BlockSpec((s, HEAD_DIM), lambda h, i, nh=n_heads: (0, nh + h)),
                  pl.BlockSpec((s, HEAD_DIM), lambda h, i, nh=n_heads: (0, 2 * nh + h))],
        out_specs=pl.BlockSpec((tq, HEAD_DIM), lambda h, i: (i, h)),
        out_shape=jax.ShapeDtypeStruct((s, n_heads * HEAD_DIM), BF16),
        scratch_shapes=[pltpu.VMEM((tq, HEAD_DIM), F32), pltpu.VMEM((tq, tq), F32)],
        compiler_params=_params(("parallel", "arbitrary"), nbytes,
                                scratch + 12 * _nbytes((tq, 2 * tq), F32)),
        name="sb_attention",
    )(qkv, qkv, qkv)


def _pool_kernel(u_ref, halo_ref, w_ref, scale_ref, o_ref, ext_ref, *, bm, gd, halo):
    i = pl.program_id(0)
    ext_ref[pl.ds(0, halo), :] = jnp.where(i > 0, halo_ref[...], 0.0)
    ext_ref[pl.ds(halo, bm), :] = u_ref[...]
    pos = i * bm + lax.broadcasted_iota(jnp.int32, (bm, gd), 0)
    for g, win in enumerate(POOL_WINDOWS):
        cols = pl.ds(g * gd, gd)
        tok = ext_ref[pl.ds(halo, bm), cols]
        total = tok
        for back in range(1, win):
            total = total + ext_ref[pl.ds(halo - back, bm), cols]
        count = jnp.minimum(pos + 1, win).astype(F32)
        pooled = (total / count - tok).astype(BF16)
        mixed = jnp.dot(pooled, w_ref[g], preferred_element_type=F32)
        o_ref[:, cols] = (mixed * scale_ref[:, cols]).astype(o_ref.dtype)


def _multiscale_pool(u, w_groups, pool_scale):
    s, pw = u.shape
    n_groups, gd, _ = w_groups.shape
    assert n_groups == len(POOL_WINDOWS)
    halo = 16
    assert halo >= max(POOL_WINDOWS) and s % halo == 0
    bm = _tile(s, 512)
    hb = bm // halo
    nbytes = (_nbytes((bm, pw), F32) + _nbytes((halo, pw), F32) + _nbytes(w_groups.shape, BF16)
              + _nbytes((bm, pw), BF16))
    scratch = _nbytes((bm + halo, pw), F32)
    return pl.pallas_call(
        functools.partial(_pool_kernel, bm=bm, gd=gd, halo=halo),
        grid=(s // bm,),
        in_specs=[pl.BlockSpec((bm, pw), lambda i: (i, 0)),
                  pl.BlockSpec((halo, pw), lambda i, hb=hb: (jnp.maximum(i * hb - 1, 0), 0)),
                  pl.BlockSpec((n_groups, gd, gd), lambda i: (0, 0, 0)),
                  pl.BlockSpec((1, pw), lambda i: (0, 0))],
        out_specs=pl.BlockSpec((bm, pw), lambda i: (i, 0)),
        out_shape=jax.ShapeDtypeStruct((s, pw), BF16),
        scratch_shapes=[pltpu.VMEM((bm + halo, pw), F32)],
        compiler_params=_params(("parallel",), nbytes, scratch + 4 * _nbytes((bm, gd), F32)),
        name="multiscale_pool",
    )(u, u, w_groups, pool_scale.reshape(1, pw))


def _xattn_kernel(q_ref, kv_ref, o_ref, *, n_heads, scale):
    width = n_heads * HEAD_DIM
    for h in range(n_heads):
        cols = pl.ds(h * HEAD_DIM, HEAD_DIM)
        q = q_ref[:, cols]
        k = kv_ref[:, cols]
        v = kv_ref[:, pl.ds(width + h * HEAD_DIM, HEAD_DIM)]
        s = lax.dot_general(q, k, (((1,), (1,)), ((), ())), preferred_element_type=F32) * scale
        e = jnp.exp(s - jnp.max(s, axis=-1, keepdims=True))
        p = e / jnp.sum(e, axis=-1, keepdims=True)
        o_ref[:, cols] = jnp.dot(p.astype(BF16), v, preferred_element_type=F32).astype(o_ref.dtype)


def _cross_attention(q, kv, *, n_heads):
    s, width = q.shape
    m = kv.shape[0]
    bm = _tile(s, 512)
    nbytes = 2 * _nbytes((bm, width), BF16) + _nbytes(kv.shape, BF16)
    return pl.pallas_call(
        functools.partial(_xattn_kernel, n_heads=n_heads, scale=HEAD_DIM ** -0.5),
        grid=(s // bm,),
        in_specs=[pl.BlockSpec((bm, width), lambda i: (i, 0)),
                  pl.BlockSpec((m, 2 * width), lambda i: (0, 0))],
        out_specs=pl.BlockSpec((bm, width), lambda i: (i, 0)),
        out_shape=jax.ShapeDtypeStruct((s, width), BF16),
        compiler_params=_params(("parallel",), nbytes, 6 * _nbytes((bm, m), F32)),
        name="cross_attention",
    )(q, kv)


def _topk_rows(scores, k, rank_scale):
    n, w = scores.shape
    rows = lax.broadcasted_iota(jnp.int32, (n, w), 0).astype(F32)
    krows = lax.broadcasted_iota(jnp.int32, (k, w), 0)

    def body(r, carry):
        work, rank, vals, _ = carry
        top = jnp.max(work, axis=0, keepdims=True)
        idx = jnp.min(jnp.where(work == top, rows, float(n)), axis=0, keepdims=True)
        hit = rows == idx
        rank = jnp.where(hit, r.astype(F32) * rank_scale, rank)
        work = jnp.where(hit, -jnp.inf, work)
        vals = jnp.where(krows == r, top, vals)
        return work, rank, vals, idx

    init = (scores, jnp.full((n, w), RANK_NONE, F32), jnp.zeros((k, w), F32), jnp.zeros((1, w), F32))
    _, rank, vals, last = lax.fori_loop(0, k, body, init)
    return vals, rank, last


def _peer_select_kernel(q_ref, keys_ref, s1_ref, s2_ref, a_ref, b_ref, ra_ref, rb_ref, tf_ref,
                        *, half, tn):
    k = PEER_TOPK
    for cg in range(tn // LANES):
        lanes = pl.ds(cg * LANES, LANES)
        s1 = jnp.dot(keys_ref[0], q_ref[pl.ds(0, half), lanes], preferred_element_type=F32)
        s2 = jnp.dot(keys_ref[1], q_ref[pl.ds(half, half), lanes], preferred_element_type=F32)
        v1, ra, _ = _topk_rows(s1, k, float(k))
        v2, rb, _ = _topk_rows(s2, k, 1.0)
        cand = jnp.concatenate([v1[a:a + 1, :] + v2 for a in range(k)], axis=0)
        best, _, last = _topk_rows(cand, k, 1.0)
        tau = best[k - 1:k, :]
        z = jnp.sum(jnp.exp(best - best[0:1, :]), axis=0, keepdims=True)
        s1_ref[:, lanes] = s1
        s2_ref[:, lanes] = s2
        a_ref[:, lanes] = jnp.exp(s1 - v1[0:1, :])
        b_ref[:, lanes] = jnp.exp(s2 - v2[0:1, :]) / z
        ra_ref[:, lanes] = ra
        rb_ref[:, lanes] = rb
        tf_ref[:, lanes] = jnp.concatenate(
            [tau, last, jnp.zeros((SUBLANES - 2, LANES), F32)], axis=0)


def _peer_select(q_t, sub_keys):
    n_heads, _, n_keys, half = sub_keys.shape
    s = q_t.shape[1]
    tn = _tile(s, 512)
    dense = jax.ShapeDtypeStruct((n_heads, n_keys, s), F32)
    dense_spec = pl.BlockSpec((None, n_keys, tn), lambda i, h: (h, 0, i))
    nbytes = (_nbytes((2 * half, tn), BF16) + _nbytes((2, n_keys, half), BF16)
              + 6 * _nbytes((n_keys, tn), F32) + _nbytes((SUBLANES, tn), F32))
    return pl.pallas_call(
        functools.partial(_peer_select_kernel, half=half, tn=tn),
        grid=(s // tn, n_heads),
        in_specs=[pl.BlockSpec((2 * half, tn), lambda i, h: (h, i)),
                  pl.BlockSpec((None, 2, n_keys, half), lambda i, h: (h, 0, 0, 0))],
        out_specs=[dense_spec] * 6 + [pl.BlockSpec((None, SUBLANES, tn), lambda i, h: (h, 0, i))],
        out_shape=[dense] * 6 + [jax.ShapeDtypeStruct((n_heads, SUBLANES, s), F32)],
        compiler_params=_params(("parallel", "parallel"), nbytes, 8 * 2**20),
        name="peer_select",
    )(q_t, sub_keys)


def _gelu(x):
    return 0.5 * x * (1.0 + lax.erf(x * (2.0 ** -0.5)))


def _peer_dense_kernel(ht_ref, dn_ref, upt_ref, s1_ref, a_ref, ra_ref, s2_ref, b_ref, rb_ref,
                       tf_ref, o_ref, pt_ref, wt_ref, *, n_heads, n_keys, te, tn, irows, jc):
    e = pl.program_id(1)
    rps = te // n_keys
    ioff = (e * rps) % irows
    pt_ref[...] = jnp.dot(dn_ref[...], ht_ref[...], preferred_element_type=F32)
    chunks_per_i = n_keys // jc

    def chunk(c, carry):
        irow = ioff + c // chunks_per_i
        j0 = pl.multiple_of((c % chunks_per_i) * jc, jc)
        r0 = pl.multiple_of(c * jc, jc)
        gate = jnp.zeros((jc, tn), F32)
        for h in range(n_heads):
            s1 = s1_ref[h, pl.ds(irow, 1), :]
            a1 = a_ref[h, pl.ds(irow, 1), :]
            ra = ra_ref[h, pl.ds(irow, 1), :]
            tau = tf_ref[h, 0:1, :]
            last = tf_ref[h, 1:2, :]
            csum = s1 + s2_ref[h, pl.ds(j0, jc), :]
            order = ra + rb_ref[h, pl.ds(j0, jc), :]
            prod = a1 * b_ref[h, pl.ds(j0, jc), :]
            tie = jnp.where(csum == tau, jnp.where(order <= last, prod, 0.0), 0.0)
            gate = gate + jnp.where(csum > tau, prod, tie)
        act = _gelu(pt_ref[pl.ds(r0, jc), :])
        wt_ref[pl.ds(r0, jc), :] = (gate * act).astype(wt_ref.dtype)
        return carry

    lax.fori_loop(0, te // jc, chunk, 0)
    y = jnp.dot(upt_ref[...], wt_ref[...], preferred_element_type=F32)

    @pl.when(e == 0)
    def _():
        o_ref[...] = y

    @pl.when(e > 0)
    def _():
        o_ref[...] += y


def _peer_dense(h_t, down, up_t, sel):
    s1, s2, a, b, ra, rb, tf = sel
    n_heads, n_keys, s = s1.shape
    d = h_t.shape[0]
    n_exp = down.shape[0]
    tn = _tile(s, 512)
    te = _tile(n_exp, 512)
    assert te % n_keys == 0
    rps = te // n_keys
    irows = max(rps, SUBLANES)
    assert irows % rps == 0 and n_keys % irows == 0
    jc = 16
    once = pl.Buffered(1)
    by_i = pl.BlockSpec((n_heads, irows, tn), lambda i, e: (0, (e * rps) // irows, i))
    by_j = pl.BlockSpec((n_heads, n_keys, tn), lambda i, e: (0, 0, i), pipeline_mode=once)
    nbytes = (2 * _nbytes((te, d), BF16) + 3 * _nbytes((n_heads, irows, tn), F32)
              + _nbytes((d, tn), F32))
    scratch = (_nbytes((te, tn), F32) + _nbytes((te, tn), BF16) + _nbytes((d, tn), BF16)
               + 3 * _nbytes((n_heads, n_keys, tn), F32) + _nbytes((n_heads, SUBLANES, tn), F32))
    return pl.pallas_call(
        functools.partial(_peer_dense_kernel, n_heads=n_heads, n_keys=n_keys, te=te, tn=tn,
                          irows=irows, jc=jc),
        grid=(s // tn, n_exp // te),
        in_specs=[pl.BlockSpec((d, tn), lambda i, e: (0, i), pipeline_mode=once),
                  pl.BlockSpec((te, d), lambda i, e: (e, 0)),
                  pl.BlockSpec((d, te), lambda i, e: (0, e)),
                  by_i, by_i, by_i, by_j, by_j, by_j,
                  pl.BlockSpec((n_heads, SUBLANES, tn), lambda i, e: (0, 0, i), pipeline_mode=once)],
        out_specs=pl.BlockSpec((d, tn), lambda i, e: (0, i)),
        out_shape=jax.ShapeDtypeStruct((d, s), F32),
        scratch_shapes=[pltpu.VMEM((te, tn), F32), pltpu.VMEM((te, tn), BF16)],
        compiler_params=_params(("parallel", "arbitrary"), nbytes, scratch),
        name="peer_dense",
    )(h_t, down, up_t, s1, a, ra, s2, b, rb, tf)


def _add_t_kernel(x_ref, yt_ref, g_ref, o_ref, *, normalize):
    x = x_ref[...] + yt_ref[...].T
    if normalize:
        x = _rmsnorm_rows(x, g_ref[...])
    o_ref[...] = x


def _add_transposed(x, y_t, g, *, normalize):
    s, d = x.shape
    bm = _tile(s, 256)
    nbytes = 3 * _nbytes((bm, d), F32)
    return pl.pallas_call(
        functools.partial(_add_t_kernel, normalize=normalize),
        grid=(s // bm,),
        in_specs=[pl.BlockSpec((bm, d), lambda i: (i, 0)),
                  pl.BlockSpec((d, bm), lambda i: (0, i)),
                  pl.BlockSpec((1, d), lambda i: (0, 0))],
        out_specs=pl.BlockSpec((bm, d), lambda i: (i, 0)),
        out_shape=jax.ShapeDtypeStruct((s, d), F32),
        compiler_params=_params(("parallel",), nbytes, 2 * _nbytes((bm, d), F32)),
        name="add_transposed",
    )(x, y_t, g.reshape(1, d))


def _layer(x, mem, p, *, last, norm_final):
    d = x.shape[1]
    sb_w = p["w_branch_sb"].shape[0]
    pool_w = p["w_branch_pool"].shape[0]
    n_sb_heads = sb_w // HEAD_DIM
    xa_w = p["xa_w_q"].shape[1]
    bf = lambda w: w.astype(BF16)

    w_in = p["w_in"]
    h1 = _rmsnorm(x, p["norm_mix"], out_dtype=BF16)
    qkv = _matmul(h1, bf(w_in[:, :3 * sb_w]), out_dtype=BF16)
    u_pool = _matmul(h1, bf(w_in[:, 3 * sb_w:3 * sb_w + pool_w]), out_dtype=F32)
    gates = _matmul(h1, bf(w_in[:, 3 * sb_w + pool_w:]), out_dtype=F32, epilogue="sigmoid")
    attn = _sb_attention(qkv, n_heads=n_sb_heads)
    pooled = _multiscale_pool(u_pool, bf(p["pool_group_w"]), p["pool_scale"])
    mix = _gated_mix(attn, pooled, bf(p["w_branch_sb"]), bf(p["w_branch_pool"]), gates, d_model=d)
    x1 = _matmul(mix, bf(p["w_out"]), out_dtype=F32, epilogue="residual", residual=x)

    hq = _rmsnorm(x1, p["norm_mem_q"], out_dtype=BF16)
    mem_n = _rmsnorm(mem, p["norm_mem_kv"], out_dtype=BF16)
    xq = _matmul(hq, bf(p["xa_w_q"]), out_dtype=BF16)
    xkv = _matmul(mem_n, bf(p["xa_w_kv"]), out_dtype=BF16)
    xo = _cross_attention(xq, xkv, n_heads=xa_w // HEAD_DIM)
    x2 = _matmul(xo, bf(p["xa_w_o"]), out_dtype=F32, epilogue="residual", residual=x1)

    h3_t = _rmsnorm(x2, p["norm_ffn"], out_dtype=BF16, transpose=True)
    q_t = _matmul(bf(p["peer_w_query"].T), h3_t, out_dtype=BF16)
    sel = _peer_select(q_t, bf(p["peer_sub_keys"]))
    y_t = _peer_dense(h3_t, bf(p["peer_down"]), bf(p["peer_up"].T), sel)
    return _add_transposed(x2, y_t, norm_final, normalize=last)


_LAYER_PARAMS = ("norm_mix", "norm_mem_q", "norm_mem_kv", "norm_ffn", "w_in", "pool_group_w",
                 "pool_scale", "w_branch_sb", "w_branch_pool", "w_out", "xa_w_q", "xa_w_kv",
                 "xa_w_o", "peer_w_query", "peer_sub_keys", "peer_down", "peer_up")


def kernel(x, mem, norm_mix, norm_mem_q, norm_mem_kv, norm_ffn, norm_final, w_in, pool_group_w,
           pool_scale, w_branch_sb, w_branch_pool, w_out, xa_w_q, xa_w_kv, xa_w_o,
           peer_w_query, peer_sub_keys, peer_down, peer_up):
    stacked = dict(norm_mix=norm_mix, norm_mem_q=norm_mem_q, norm_mem_kv=norm_mem_kv,
                   norm_ffn=norm_ffn, w_in=w_in, pool_group_w=pool_group_w, pool_scale=pool_scale,
                   w_branch_sb=w_branch_sb, w_branch_pool=w_branch_pool, w_out=w_out,
                   xa_w_q=xa_w_q, xa_w_kv=xa_w_kv, xa_w_o=xa_w_o, peer_w_query=peer_w_query,
                   peer_sub_keys=peer_sub_keys, peer_down=peer_down, peer_up=peer_up)
    depth = w_in.shape[0]
    outs = []
    for b in range(x.shape[0]):
        xb = x[b]
        for l in range(depth):
            p = {name: stacked[name][l] for name in _LAYER_PARAMS}
            xb = _layer(xb, mem[b], p, last=(l == depth - 1), norm_final=norm_final)
        outs.append(xb)
    return jnp.stack(outs, axis=0)
```

```python
import functools
import math

import jax
import jax.numpy as jnp
from jax import lax
from jax.experimental import pallas as pl
from jax.experimental.pallas import tpu as pltpu

F32 = jnp.float32
BF16 = jnp.bfloat16

HEAD_DIM = 128
POOL_WINDOWS = (2, 4, 8, 16)
PEER_TOPK = 16
EPS = 1e-6

V7X_VMEM_BYTES = 64 * 2**20
V7X_VMEM_BUDGET = 56 * 2**20
LANES = 128
SUBLANES = 8

SB_DECAY_STOP = 120.0
RANK_NONE = 1.0e9


def _vmem_limit(pipelined_bytes, scratch_bytes=0):
    need = 2 * pipelined_bytes + scratch_bytes + 12 * 2**20
    return int(min(V7X_VMEM_BUDGET, max(need, 16 * 2**20)))


def _nbytes(shape, dtype):
    return math.prod(shape) * jnp.dtype(dtype).itemsize


def _params(semantics, pipelined_bytes, scratch_bytes=0, flags=None):
    return pltpu.CompilerParams(
        dimension_semantics=semantics, flags=flags,
        vmem_limit_bytes=_vmem_limit(pipelined_bytes, scratch_bytes))


def _tile(n, pref):
    t = min(n, pref)
    assert n % t == 0, (n, pref)
    return t


def _rmsnorm_rows(x, g):
    return x * lax.rsqrt(jnp.mean(x * x, axis=-1, keepdims=True) + EPS) * g


def _rmsnorm_kernel(x_ref, g_ref, o_ref, *, transpose):
    y = _rmsnorm_rows(x_ref[...], g_ref[...])
    if transpose:
        y = y.T
    o_ref[...] = y.astype(o_ref.dtype)


def _rmsnorm(x, g, *, out_dtype, transpose=False):
    m, d = x.shape
    bm = _tile(m, 256)
    if transpose:
        out_shape = jax.ShapeDtypeStruct((d, m), out_dtype)
        out_spec = pl.BlockSpec((d, bm), lambda i: (0, i))
    else:
        out_shape = jax.ShapeDtypeStruct((m, d), out_dtype)
        out_spec = pl.BlockSpec((bm, d), lambda i: (i, 0))
    nbytes = _nbytes((bm, d), F32) + _nbytes((bm, d), out_dtype)
    return pl.pallas_call(
        functools.partial(_rmsnorm_kernel, transpose=transpose),
        grid=(m // bm,),
        in_specs=[pl.BlockSpec((bm, d), lambda i: (i, 0)),
                  pl.BlockSpec((1, d), lambda i: (0, 0))],
        out_specs=out_spec,
        out_shape=out_shape,
        compiler_params=_params(("parallel",), nbytes, 2 * _nbytes((bm, d), F32)),
        name="rmsnorm",
    )(x, g.reshape(1, d))


def _mm_kernel(a_ref, w_ref, *rest, epilogue):
    o_ref = rest[-1]
    acc = jnp.dot(a_ref[...], w_ref[...], preferred_element_type=F32)
    if epilogue == "sigmoid":
        acc = jax.nn.sigmoid(acc)
    elif epilogue == "residual":
        acc = rest[0][...] + acc
    o_ref[...] = acc.astype(o_ref.dtype)


def _matmul(a, w, *, out_dtype, epilogue="none", residual=None, bm=1024, bn=1024,
            col_start=0, n_cols=None):
    m, k = a.shape
    k2, n_all = w.shape
    assert k == k2
    n = n_all - col_start if n_cols is None else n_cols
    bm = _tile(m, bm)
    bn = next(t for t in (bn, bn // 2, bn // 4, LANES) if n % t == 0 and col_start % t == 0)
    j0 = col_start // bn
    in_specs = [pl.BlockSpec((bm, k), lambda i, j: (i, 0)),
                pl.BlockSpec((k, bn), lambda i, j: (0, j0 + j))]
    args = [a, w]
    nbytes = _nbytes((bm, k), a.dtype) + _nbytes((k, bn), w.dtype) + _nbytes((bm, bn), out_dtype)
    if epilogue == "residual":
        in_specs.append(pl.BlockSpec((bm, bn), lambda i, j: (i, j)))
        args.append(residual)
        nbytes += _nbytes((bm, bn), residual.dtype)
    return pl.pallas_call(
        functools.partial(_mm_kernel, epilogue=epilogue),
        grid=(m // bm, n // bn),
        in_specs=in_specs,
        out_specs=pl.BlockSpec((bm, bn), lambda i, j: (i, j)),
        out_shape=jax.ShapeDtypeStruct((m, n), out_dtype),
        compiler_params=_params(("parallel", "parallel"), nbytes, _nbytes((bm, bn), F32)),
        name="matmul_" + epilogue,
    )(*args)


def _mix_kernel(a1_ref, a2_ref, w1_ref, w2_ref, g1_ref, g2_ref, o_ref):
    y1 = jnp.dot(a1_ref[...], w1_ref[...], preferred_element_type=F32)
    y2 = jnp.dot(a2_ref[...], w2_ref[...], preferred_element_type=F32)
    o_ref[...] = (g1_ref[...] * y1 + g2_ref[...] * y2).astype(o_ref.dtype)


def _gated_mix(attn, pooled, w_sb, w_pool, gates, *, d_model):
    s, k1 = attn.shape
    k2 = pooled.shape[1]
    bm = _tile(s, 1024)
    bn = _tile(d_model, 512)
    nj = d_model // bn
    nbytes = (_nbytes((bm, k1), BF16) + _nbytes((bm, k2), BF16) + _nbytes((k1, bn), BF16)
              + _nbytes((k2, bn), BF16) + 2 * _nbytes((bm, bn), F32) + _nbytes((bm, bn), BF16))
    return pl.pallas_call(
        _mix_kernel,
        grid=(s // bm, nj),
        in_specs=[pl.BlockSpec((bm, k1), lambda i, j: (i, 0)),
                  pl.BlockSpec((bm, k2), lambda i, j: (i, 0)),
                  pl.BlockSpec((k1, bn), lambda i, j: (0, j)),
                  pl.BlockSpec((k2, bn), lambda i, j: (0, j)),
                  pl.BlockSpec((bm, bn), lambda i, j: (i, j)),
                  pl.BlockSpec((bm, bn), lambda i, j, nj=nj: (i, j + nj))],
        out_specs=pl.BlockSpec((bm, bn), lambda i, j: (i, j)),
        out_shape=jax.ShapeDtypeStruct((s, d_model), BF16),
        compiler_params=_params(("parallel", "parallel"), nbytes, 3 * _nbytes((bm, bn), F32)),
        name="gated_mix",
    )(attn, pooled, w_sb, w_pool, gates, gates)


def _softplus(z):
    return jnp.maximum(z, 0.0) + jnp.log(1.0 + jnp.exp(-jnp.abs(z)))


def _sb_attn_kernel(q_ref, k_ref, v_ref, o_ref, acc_ref, carry_ref, *, tq, hp, scale):
    qi = pl.program_id(1)
    acc_ref[...] = jnp.zeros_like(acc_ref)
    carry_ref[...] = jnp.zeros_like(carry_ref)
    causal = (lax.broadcasted_iota(jnp.int32, (tq, tq), 1)
              < lax.broadcasted_iota(jnp.int32, (tq, tq), 0))
    mat_row = lax.broadcasted_iota(jnp.int32, (tq, tq + LANES), 0)
    mat_col = lax.broadcasted_iota(jnp.int32, (tq, tq + LANES), 1)
    suffix_mat = jnp.where((mat_row > mat_col) | (mat_col >= tq), 1.0, 0.0).astype(BF16)

    def block(kb, diagonal):
        ks = pl.multiple_of(kb * tq, tq)
        smallest = None
        for h in range(hp):
            cols = pl.ds(h * HEAD_DIM, HEAD_DIM)
            z = lax.dot_general(q_ref[:, cols], k_ref[pl.ds(ks, tq), cols],
                                (((1,), (1,)), ((), ())), preferred_element_type=F32) * scale
            spz = _softplus(z)
            sp = jnp.where(causal, spz, 0.0) if diagonal else spz
            sp_hi = sp.astype(BF16)
            sp_lo = (sp - sp_hi.astype(F32)).astype(BF16)
            sums = (jnp.dot(sp_hi, suffix_mat, preferred_element_type=F32)
                    + jnp.dot(sp_lo, suffix_mat, preferred_element_type=F32))
            carry = carry_ref[h]
            decay = sums[:, :tq] + jnp.concatenate([carry] * (tq // LANES), axis=1)
            a = jnp.exp(z - spz - decay)
            if diagonal:
                a = jnp.where(causal, a, 0.0)
            acc_ref[:, cols] += jnp.dot(a.astype(BF16), v_ref[pl.ds(ks, tq), cols],
                                        preferred_element_type=F32)
            new_carry = carry + sums[:, tq:]
            carry_ref[h] = new_carry
            least = jnp.min(new_carry)
            smallest = least if smallest is None else jnp.minimum(smallest, least)
        return smallest

    def cond(state):
        kb, min_carry = state
        return jnp.logical_and(kb >= 0, min_carry < SB_DECAY_STOP)

    def body(state):
        kb, _ = state
        return kb - 1, block(kb, False)

    lax.while_loop(cond, body, (qi - 1, block(qi, True)))
    o_ref[...] = acc_ref[...].astype(o_ref.dtype)


def _sb_attention(qkv, *, n_heads):
    s = qkv.shape[0]
    tq = _tile(s, 256)
    hp = next(c for c in (4, 2, 1) if n_heads % c == 0)
    ng = n_heads // hp
    gw = hp * HEAD_DIM
    nbytes = 2 * _nbytes((tq, gw), BF16) + 2 * _nbytes((s, gw), BF16)
    scratch = _nbytes((tq, gw), F32) + _nbytes((hp, tq, LANES), F32)
    return pl.pallas_call(
        functools.partial(_sb_attn_kernel, tq=tq, hp=hp, scale=HEAD_DIM ** -0.5),
        grid=(ng, s // tq),
        in_specs=[pl.BlockSpec((tq, gw), lambda g, i: (i, g)),
                  pl.BlockSpec((s, gw), lambda g, i, ng=ng: (0, ng + g)),
                  pl.BlockSpec((s, gw), lambda g, i, ng=ng: (0, 2 * ng + g))],
        out_specs=pl.BlockSpec((tq, gw), lambda g, i: (i, g)),
        out_shape=jax.ShapeDtypeStruct((s, n_heads * HEAD_DIM), BF16),
        scratch_shapes=[pltpu.VMEM((tq, gw), F32), pltpu.VMEM((hp, tq, LANES), F32)],
        compiler_params=_params(("parallel", "arbitrary"), nbytes,
                                scratch + 12 * hp * _nbytes((tq, 2 * tq), F32)),
        name="sb_attention",
    )(qkv, qkv, qkv)


def _pool_kernel(u_ref, halo_ref, w_ref, scale_ref, o_ref, ext_ref, *, bm, gd, halo):
    i = pl.program_id(0)
    ext_ref[pl.ds(0, halo), :] = jnp.where(i > 0, halo_ref[...], 0.0)
    ext_ref[pl.ds(halo, bm), :] = u_ref[...]
    pos = i * bm + lax.broadcasted_iota(jnp.int32, (bm, gd), 0)
    for g, win in enumerate(POOL_WINDOWS):
        cols = pl.ds(g * gd, gd)
        tok = ext_ref[pl.ds(halo, bm), cols]
        total = tok
        for back in range(1, win):
            total = total + ext_ref[pl.ds(halo - back, bm), cols]
        count = jnp.minimum(pos + 1, win).astype(F32)
        pooled = (total / count - tok).astype(BF16)
        mixed = jnp.dot(pooled, w_ref[g], preferred_element_type=F32)
        o_ref[:, cols] = (mixed * scale_ref[:, cols]).astype(o_ref.dtype)


def _multiscale_pool(u, w_groups, pool_scale):
    s, pw = u.shape
    n_groups, gd, _ = w_groups.shape
    assert n_groups == len(POOL_WINDOWS)
    halo = 16
    assert halo >= max(POOL_WINDOWS) and s % halo == 0
    bm = _tile(s, 512)
    hb = bm // halo
    nbytes = (_nbytes((bm, pw), F32) + _nbytes((halo, pw), F32) + _nbytes(w_groups.shape, BF16)
              + _nbytes((bm, pw), BF16))
    scratch = _nbytes((bm + halo, pw), F32)
    return pl.pallas_call(
        functools.partial(_pool_kernel, bm=bm, gd=gd, halo=halo),
        grid=(s // bm,),
        in_specs=[pl.BlockSpec((bm, pw), lambda i: (i, 0)),
                  pl.BlockSpec((halo, pw), lambda i, hb=hb: (jnp.maximum(i * hb - 1, 0), 0)),
                  pl.BlockSpec((n_groups, gd, gd), lambda i: (0, 0, 0)),
                  pl.BlockSpec((1, pw), lambda i: (0, 0))],
        out_specs=pl.BlockSpec((bm, pw), lambda i: (i, 0)),
        out_shape=jax.ShapeDtypeStruct((s, pw), BF16),
        scratch_shapes=[pltpu.VMEM((bm + halo, pw), F32)],
        compiler_params=_params(("parallel",), nbytes, scratch + 4 * _nbytes((bm, gd), F32)),
        name="multiscale_pool",
    )(u, u, w_groups, pool_scale.reshape(1, pw))


def _xattn_kernel(q_ref, kv_ref, o_ref, *, n_heads, scale):
    width = n_heads * HEAD_DIM
    for h in range(n_heads):
        cols = pl.ds(h * HEAD_DIM, HEAD_DIM)
        q = q_ref[:, cols]
        k = kv_ref[:, cols]
        v = kv_ref[:, pl.ds(width + h * HEAD_DIM, HEAD_DIM)]
        s = lax.dot_general(q, k, (((1,), (1,)), ((), ())), preferred_element_type=F32) * scale
        e = jnp.exp(s - jnp.max(s, axis=-1, keepdims=True))
        p = e / jnp.sum(e, axis=-1, keepdims=True)
        o_ref[:, cols] = jnp.dot(p.astype(BF16), v, preferred_element_type=F32).astype(o_ref.dtype)


def _cross_attention(q, kv, *, n_heads):
    s, width = q.shape
    m = kv.shape[0]
    bm = _tile(s, 512)
    nbytes = 2 * _nbytes((bm, width), BF16) + _nbytes(kv.shape, BF16)
    return pl.pallas_call(
        functools.partial(_xattn_kernel, n_heads=n_heads, scale=HEAD_DIM ** -0.5),
        grid=(s // bm,),
        in_specs=[pl.BlockSpec((bm, width), lambda i: (i, 0)),
                  pl.BlockSpec((m, 2 * width), lambda i: (0, 0))],
        out_specs=pl.BlockSpec((bm, width), lambda i: (i, 0)),
        out_shape=jax.ShapeDtypeStruct((s, width), BF16),
        compiler_params=_params(("parallel",), nbytes, 6 * _nbytes((bm, m), F32)),
        name="cross_attention",
    )(q, kv)


def _topk_rows(score_list, k):
    def body(r, carry):
        out = []
        for work, vals, idxs in carry:
            n, w = work.shape
            rows = lax.broadcasted_iota(jnp.int32, (n, w), 0).astype(F32)
            krows = lax.broadcasted_iota(jnp.int32, (k, w), 0)
            top = jnp.max(work, axis=0, keepdims=True)
            idx = jnp.min(jnp.where(work == top, rows, float(n)), axis=0, keepdims=True)
            work = jnp.where(rows == idx, -jnp.inf, work)
            vals = jnp.where(krows == r, top, vals)
            idxs = jnp.where(krows == r, idx, idxs)
            out.append((work, vals, idxs))
        return tuple(out)

    init = tuple((s, jnp.zeros((k, s.shape[1]), F32), jnp.zeros((k, s.shape[1]), F32))
                 for s in score_list)
    return [(vals, idxs) for _, vals, idxs in lax.fori_loop(0, k, body, init)]


def _peer_select_kernel(q_ref, keys_ref, a_ref, nb_ref, b_ref, rb_ref, *, half, tn):
    k = PEER_TOPK
    hk = k // 2
    n_keys = a_ref.shape[0]
    for cg in range(tn // LANES):
        lanes = pl.ds(cg * LANES, LANES)
        s1 = jnp.dot(keys_ref[0], q_ref[pl.ds(0, half), lanes], preferred_element_type=F32)
        s2 = jnp.dot(keys_ref[1], q_ref[pl.ds(half, half), lanes], preferred_element_type=F32)
        (v1, i1), (v2, i2) = _topk_rows([s1, s2], k)
        cand = jnp.concatenate(
            [v1[0:1, :] + v2] + [v1[a:a + 1, :] + v2[0:hk, :] for a in range(1, hk)]
            + [v1[hk:k, :] + v2[0:1, :]], axis=0)
        (best, bpos), = _topk_rows([cand], k)
        tau = best[k - 1:k, :]
        last = bpos[k - 1:k, :]
        pos = lax.broadcasted_iota(jnp.int32, cand.shape, 0).astype(F32)
        picked = jnp.where(cand > tau, 1.0,
                           jnp.where(cand == tau, jnp.where(pos <= last, 1.0, 0.0), 0.0))
        nb = [jnp.sum(picked[0:k, :], axis=0, keepdims=True)]
        nb += [jnp.sum(picked[k + (a - 1) * hk:k + a * hk, :], axis=0, keepdims=True)
               for a in range(1, hk)]
        tail = k + (hk - 1) * hk
        nb += [picked[tail + a:tail + a + 1, :] for a in range(k - hk)]
        rows = lax.broadcasted_iota(jnp.int32, (n_keys, LANES), 0).astype(F32)
        nb_dense = jnp.zeros((n_keys, LANES), F32)
        rb = jnp.full((n_keys, LANES), RANK_NONE, F32)
        for r in range(k):
            nb_dense = jnp.where(rows == i1[r:r + 1, :], nb[r], nb_dense)
            rb = jnp.where(rows == i2[r:r + 1, :], float(r), rb)
        z = jnp.sum(jnp.exp(best - best[0:1, :]), axis=0, keepdims=True)
        a_ref[:, lanes] = jnp.exp(s1 - v1[0:1, :])
        b_ref[:, lanes] = jnp.exp(s2 - v2[0:1, :]) / z
        nb_ref[:, lanes] = nb_dense
        rb_ref[:, lanes] = rb


def _peer_select(q_t, sub_keys):
    n_heads, _, n_keys, half = sub_keys.shape
    s = q_t.shape[1]
    assert PEER_TOPK % (2 * SUBLANES) == 0 and n_keys >= PEER_TOPK
    tn = _tile(s, 512)
    dense = jax.ShapeDtypeStruct((n_heads, n_keys, s), F32)
    dense_spec = pl.BlockSpec((None, n_keys, tn), lambda i, h: (h, 0, i))
    nbytes = (_nbytes((2 * half, tn), BF16) + _nbytes((2, n_keys, half), BF16)
              + 4 * _nbytes((n_keys, tn), F32))
    return pl.pallas_call(
        functools.partial(_peer_select_kernel, half=half, tn=tn),
        grid=(s // tn, n_heads),
        in_specs=[pl.BlockSpec((2 * half, tn), lambda i, h: (h, i)),
                  pl.BlockSpec((None, 2, n_keys, half), lambda i, h: (h, 0, 0, 0))],
        out_specs=[dense_spec] * 4,
        out_shape=[dense] * 4,
        compiler_params=_params(("parallel", "parallel"), nbytes, 8 * 2**20),
        name="peer_select",
    )(q_t, sub_keys)


def _gelu(x):
    return 0.5 * x * (1.0 + lax.erf(x * (2.0 ** -0.5)))


def _peer_dense_kernel(ht_ref, dn_ref, upt_ref, a_ref, nb_ref, b_ref, rb_ref, o_ref,
                       pt0_ref, pt1_ref, wt0_ref, wt1_ref,
                       *, n_heads, n_keys, n_tiles, te, tn, irows, jc, pieces):
    e = pl.program_id(1)
    rps = te // n_keys
    gate_tile = jnp.clip(e - 1, 0, n_tiles - 1)
    ioff = (gate_tile * rps) % irows
    chunks_per_i = n_keys // jc

    @pl.when(e == 0)
    def _():
        pt1_ref[...] = jnp.zeros_like(pt1_ref)
        wt0_ref[...] = jnp.zeros_like(wt0_ref)
        o_ref[...] = jnp.zeros_like(o_ref)

    def step(pt_new, pt_old, wt_new, wt_old):
        def gate_chunk(c):
            irow = ioff + c // chunks_per_i
            jrows = pl.ds((c % chunks_per_i) * jc, jc)
            gate = jnp.zeros((jc, tn), F32)
            for h in range(n_heads):
                prod = a_ref[h, pl.ds(irow, 1), :] * b_ref[h, jrows, :]
                picked = rb_ref[h, jrows, :] < nb_ref[h, pl.ds(irow, 1), :]
                gate = gate + jnp.where(picked, prod, 0.0)
            act = _gelu(pt_old[pl.ds(c * jc, jc), :])
            wt_new[pl.ds(c * jc, jc), :] = (gate * act).astype(wt_new.dtype)

        n_chunks = te // jc
        d_per = ht_ref.shape[0] // pieces
        for p in range(pieces):
            span = pl.ds(p * d_per, d_per)
            o_ref[span, :] += jnp.dot(upt_ref[span, :], wt_old[...], preferred_element_type=F32)
            for c in range(2 * p * n_chunks // (2 * pieces), (2 * p + 1) * n_chunks // (2 * pieces)):
                gate_chunk(c)
            part = jnp.dot(dn_ref[:, span], ht_ref[span, :], preferred_element_type=F32)
            if p == 0:
                pt_new[...] = part
            else:
                pt_new[...] += part
            for c in range((2 * p + 1) * n_chunks // (2 * pieces), (2 * p + 2) * n_chunks // (2 * pieces)):
                gate_chunk(c)

    @pl.when(e % 2 == 0)
    def _():
        step(pt0_ref, pt1_ref, wt1_ref, wt0_ref)

    @pl.when(e % 2 == 1)
    def _():
        step(pt1_ref, pt0_ref, wt0_ref, wt1_ref)


def _peer_dense(h_t, down, up_t, sel):
    a, nb, b, rb = sel
    n_heads, n_keys, s = a.shape
    d = h_t.shape[0]
    n_exp = down.shape[0]
    tn = _tile(s, 512)
    te = _tile(n_exp, 512)
    assert te % n_keys == 0
    n_tiles = n_exp // te
    rps = te // n_keys
    irows = max(rps, SUBLANES)
    assert irows % rps == 0 and n_keys % irows == 0
    jc = 16
    pieces = next(p for p in (16, 8, 4, 2, 1)
                  if d % (p * LANES) == 0 and (te // jc) % (2 * p) == 0)
    clamp = lambda t: jnp.clip(t, 0, n_tiles - 1)
    once = pl.Buffered(1)
    by_i = pl.BlockSpec((n_heads, irows, tn), lambda i, e: (0, (clamp(e - 1) * rps) // irows, i))
    by_j = pl.BlockSpec((n_heads, n_keys, tn), lambda i, e: (0, 0, i), pipeline_mode=once)
    nbytes = (2 * _nbytes((te, d), BF16) + 2 * _nbytes((n_heads, irows, tn), F32)
              + _nbytes((d, tn), F32))
    scratch = (2 * _nbytes((te, tn), F32) + 2 * _nbytes((te, tn), BF16) + _nbytes((d, tn), BF16)
               + 2 * _nbytes((n_heads, n_keys, tn), F32))
    return pl.pallas_call(
        functools.partial(_peer_dense_kernel, n_heads=n_heads, n_keys=n_keys, n_tiles=n_tiles,
                          te=te, tn=tn, irows=irows, jc=jc, pieces=pieces),
        grid=(s // tn, n_tiles + 2),
        in_specs=[pl.BlockSpec((d, tn), lambda i, e: (0, i), pipeline_mode=once),
                  pl.BlockSpec((te, d), lambda i, e: (clamp(e), 0)),
                  pl.BlockSpec((d, te), lambda i, e: (0, clamp(e - 2))),
                  by_i, by_i, by_j, by_j],
        out_specs=pl.BlockSpec((d, tn), lambda i, e: (0, i)),
        out_shape=jax.ShapeDtypeStruct((d, s), F32),
        scratch_shapes=[pltpu.VMEM((te, tn), F32), pltpu.VMEM((te, tn), F32),
                        pltpu.VMEM((te, tn), BF16), pltpu.VMEM((te, tn), BF16)],
        compiler_params=_params(("parallel", "arbitrary"), nbytes, scratch),
        name="peer_dense",
    )(h_t, down, up_t, a, nb, b, rb)


def _add_t_kernel(x_ref, yt_ref, g_ref, o_ref, *, normalize):
    x = x_ref[...] + yt_ref[...].T
    if normalize:
        x = _rmsnorm_rows(x, g_ref[...])
    o_ref[...] = x


def _add_transposed(x, y_t, g, *, normalize):
    s, d = x.shape
    bm = _tile(s, 256)
    nbytes = 3 * _nbytes((bm, d), F32)
    return pl.pallas_call(
        functools.partial(_add_t_kernel, normalize=normalize),
        grid=(s // bm,),
        in_specs=[pl.BlockSpec((bm, d), lambda i: (i, 0)),
                  pl.BlockSpec((d, bm), lambda i: (0, i)),
                  pl.BlockSpec((1, d), lambda i: (0, 0))],
        out_specs=pl.BlockSpec((bm, d), lambda i: (i, 0)),
        out_shape=jax.ShapeDtypeStruct((s, d), F32),
        compiler_params=_params(("parallel",), nbytes, 2 * _nbytes((bm, d), F32)),
        name="add_transposed",
    )(x, y_t, g.reshape(1, d))


def _layer(x, mem, p, *, last, norm_final):
    d = x.shape[1]
    sb_w = p["w_branch_sb"].shape[0]
    pool_w = p["w_branch_pool"].shape[0]
    n_sb_heads = sb_w // HEAD_DIM
    xa_w = p["xa_w_q"].shape[1]
    bf = lambda w: w.astype(BF16)

    w_in = bf(p["w_in"])
    h1 = _rmsnorm(x, p["norm_mix"], out_dtype=BF16)
    qkv = _matmul(h1, w_in, out_dtype=BF16, n_cols=3 * sb_w)
    u_pool = _matmul(h1, w_in, out_dtype=F32, col_start=3 * sb_w, n_cols=pool_w)
    gates = _matmul(h1, w_in, out_dtype=F32, epilogue="sigmoid", col_start=3 * sb_w + pool_w)
    attn = _sb_attention(qkv, n_heads=n_sb_heads)
    pooled = _multiscale_pool(u_pool, bf(p["pool_group_w"]), p["pool_scale"])
    mix = _gated_mix(attn, pooled, bf(p["w_branch_sb"]), bf(p["w_branch_pool"]), gates, d_model=d)
    x1 = _matmul(mix, bf(p["w_out"]), out_dtype=F32, epilogue="residual", residual=x)

    hq = _rmsnorm(x1, p["norm_mem_q"], out_dtype=BF16)
    mem_n = _rmsnorm(mem, p["norm_mem_kv"], out_dtype=BF16)
    xq = _matmul(hq, bf(p["xa_w_q"]), out_dtype=BF16)
    xkv = _matmul(mem_n, bf(p["xa_w_kv"]), out_dtype=BF16)
    xo = _cross_attention(xq, xkv, n_heads=xa_w // HEAD_DIM)
    x2 = _matmul(xo, bf(p["xa_w_o"]), out_dtype=F32, epilogue="residual", residual=x1)

    h3_t = _rmsnorm(x2, p["norm_ffn"], out_dtype=BF16, transpose=True)
    q_t = _matmul(bf(p["peer_w_query"].T), h3_t, out_dtype=BF16)
    sel = _peer_select(q_t, bf(p["peer_sub_keys"]))
    y_t = _peer_dense(h3_t, bf(p["peer_down"]), bf(p["peer_up"].T), sel)
    return _add_transposed(x2, y_t, norm_final, normalize=last)


_LAYER_PARAMS = ("norm_mix", "norm_mem_q", "norm_mem_kv", "norm_ffn", "w_in", "pool_group_w",
                 "pool_scale", "w_branch_sb", "w_branch_pool", "w_out", "xa_w_q", "xa_w_kv",
                 "xa_w_o", "peer_w_query", "peer_sub_keys", "peer_down", "peer_up")


def kernel(x, mem, norm_mix, norm_mem_q, norm_mem_kv, norm_ffn, norm_final, w_in, pool_group_w,
           pool_scale, w_branch_sb, w_branch_pool, w_out, xa_w_q, xa_w_kv, xa_w_o,
           peer_w_query, peer_sub_keys, peer_down, peer_up):
    stacked = dict(norm_mix=norm_mix, norm_mem_q=norm_mem_q, norm_mem_kv=norm_mem_kv,
                   norm_ffn=norm_ffn, w_in=w_in, pool_group_w=pool_group_w, pool_scale=pool_scale,
                   w_branch_sb=w_branch_sb, w_branch_pool=w_branch_pool, w_out=w_out,
                   xa_w_q=xa_w_q, xa_w_kv=xa_w_kv, xa_w_o=xa_w_o, peer_w_query=peer_w_query,
                   peer_sub_keys=peer_sub_keys, peer_down=peer_down, peer_up=peer_up)
    depth = w_in.shape[0]
    outs = []
    for b in range(x.shape[0]):
        xb = x[b]
        for l in range(depth):
            p = {name: stacked[name][l] for name in _LAYER_PARAMS}
            xb = _layer(xb, mem[b], p, last=(l == depth - 1), norm_final=norm_final)
        outs.append(xb)
    return jnp.stack(outs, axis=0)
```

```python
import functools
import math

import jax
import jax.numpy as jnp
from jax import lax
from jax.experimental import pallas as pl
from jax.experimental.pallas import tpu as pltpu

F32 = jnp.float32
BF16 = jnp.bfloat16

HEAD_DIM = 128
POOL_WINDOWS = (2, 4, 8, 16)
PEER_TOPK = 16
EPS = 1e-6

V7X_VMEM_BYTES = 64 * 2**20
V7X_VMEM_BUDGET = 56 * 2**20
LANES = 128
SUBLANES = 8

SB_DECAY_STOP = 120.0
RANK_NONE = 1.0e9


def _vmem_limit(pipelined_bytes, scratch_bytes=0):
    need = 2 * pipelined_bytes + scratch_bytes + 12 * 2**20
    return int(min(V7X_VMEM_BUDGET, max(need, 16 * 2**20)))


def _nbytes(shape, dtype):
    return math.prod(shape) * jnp.dtype(dtype).itemsize


def _params(semantics, pipelined_bytes, scratch_bytes=0, flags=None):
    return pltpu.CompilerParams(
        dimension_semantics=semantics, flags=flags,
        vmem_limit_bytes=_vmem_limit(pipelined_bytes, scratch_bytes))


def _tile(n, pref):
    t = min(n, pref)
    assert n % t == 0, (n, pref)
    return t


def _rmsnorm_rows(x, g):
    return x * lax.rsqrt(jnp.mean(x * x, axis=-1, keepdims=True) + EPS) * g


def _rmsnorm_kernel(x_ref, g_ref, o_ref, *, transpose):
    y = _rmsnorm_rows(x_ref[...], g_ref[...])
    if transpose:
        y = y.T
    o_ref[...] = y.astype(o_ref.dtype)


def _rmsnorm(x, g, *, out_dtype, transpose=False):
    m, d = x.shape
    bm = _tile(m, 256)
    if transpose:
        out_shape = jax.ShapeDtypeStruct((d, m), out_dtype)
        out_spec = pl.BlockSpec((d, bm), lambda i: (0, i))
    else:
        out_shape = jax.ShapeDtypeStruct((m, d), out_dtype)
        out_spec = pl.BlockSpec((bm, d), lambda i: (i, 0))
    nbytes = _nbytes((bm, d), F32) + _nbytes((bm, d), out_dtype)
    return pl.pallas_call(
        functools.partial(_rmsnorm_kernel, transpose=transpose),
        grid=(m // bm,),
        in_specs=[pl.BlockSpec((bm, d), lambda i: (i, 0)),
                  pl.BlockSpec((1, d), lambda i: (0, 0))],
        out_specs=out_spec,
        out_shape=out_shape,
        compiler_params=_params(("parallel",), nbytes, 2 * _nbytes((bm, d), F32)),
        name="rmsnorm",
    )(x, g.reshape(1, d))


def _mm_kernel(a_ref, w_ref, *rest, epilogue):
    o_ref = rest[-1]
    acc = jnp.dot(a_ref[...], w_ref[...], preferred_element_type=F32)
    if epilogue == "sigmoid":
        acc = jax.nn.sigmoid(acc)
    elif epilogue == "residual":
        acc = rest[0][...] + acc
    o_ref[...] = acc.astype(o_ref.dtype)


def _matmul(a, w, *, out_dtype, epilogue="none", residual=None, bm=1024, bn=1024,
            col_start=0, n_cols=None):
    m, k = a.shape
    k2, n_all = w.shape
    assert k == k2
    n = n_all - col_start if n_cols is None else n_cols
    bm = _tile(m, bm)
    bn = next(t for t in (bn, bn // 2, bn // 4, LANES) if n % t == 0 and col_start % t == 0)
    j0 = col_start // bn
    in_specs = [pl.BlockSpec((bm, k), lambda i, j: (i, 0)),
                pl.BlockSpec((k, bn), lambda i, j: (0, j0 + j))]
    args = [a, w]
    nbytes = _nbytes((bm, k), a.dtype) + _nbytes((k, bn), w.dtype) + _nbytes((bm, bn), out_dtype)
    if epilogue == "residual":
        in_specs.append(pl.BlockSpec((bm, bn), lambda i, j: (i, j)))
        args.append(residual)
        nbytes += _nbytes((bm, bn), residual.dtype)
    return pl.pallas_call(
        functools.partial(_mm_kernel, epilogue=epilogue),
        grid=(m // bm, n // bn),
        in_specs=in_specs,
        out_specs=pl.BlockSpec((bm, bn), lambda i, j: (i, j)),
        out_shape=jax.ShapeDtypeStruct((m, n), out_dtype),
        compiler_params=_params(("parallel", "parallel"), nbytes, _nbytes((bm, bn), F32)),
        name="matmul_" + epilogue,
    )(*args)


def _mix_kernel(a1_ref, a2_ref, w1_ref, w2_ref, g1_ref, g2_ref, o_ref):
    y1 = jnp.dot(a1_ref[...], w1_ref[...], preferred_element_type=F32)
    y2 = jnp.dot(a2_ref[...], w2_ref[...], preferred_element_type=F32)
    o_ref[...] = (g1_ref[...] * y1 + g2_ref[...] * y2).astype(o_ref.dtype)


def _gated_mix(attn, pooled, w_sb, w_pool, gates, *, d_model):
    s, k1 = attn.shape
    k2 = pooled.shape[1]
    bm = _tile(s, 1024)
    bn = _tile(d_model, 512)
    nj = d_model // bn
    nbytes = (_nbytes((bm, k1), BF16) + _nbytes((bm, k2), BF16) + _nbytes((k1, bn), BF16)
              + _nbytes((k2, bn), BF16) + 2 * _nbytes((bm, bn), F32) + _nbytes((bm, bn), BF16))
    return pl.pallas_call(
        _mix_kernel,
        grid=(s // bm, nj),
        in_specs=[pl.BlockSpec((bm, k1), lambda i, j: (i, 0)),
                  pl.BlockSpec((bm, k2), lambda i, j: (i, 0)),
                  pl.BlockSpec((k1, bn), lambda i, j: (0, j)),
                  pl.BlockSpec((k2, bn), lambda i, j: (0, j)),
                  pl.BlockSpec((bm, bn), lambda i, j: (i, j)),
                  pl.BlockSpec((bm, bn), lambda i, j, nj=nj: (i, j + nj))],
        out_specs=pl.BlockSpec((bm, bn), lambda i, j: (i, j)),
        out_shape=jax.ShapeDtypeStruct((s, d_model), BF16),
        compiler_params=_params(("parallel", "parallel"), nbytes, 3 * _nbytes((bm, bn), F32)),
        name="gated_mix",
    )(attn, pooled, w_sb, w_pool, gates, gates)


def _softplus(z):
    return jnp.maximum(z, 0.0) + jnp.log(1.0 + jnp.exp(-jnp.abs(z)))


def _sb_attn_kernel(q_ref, k_ref, v_ref, o_ref, acc_ref, carry_ref, *, tq, hp, scale):
    qi = pl.program_id(1)
    acc_ref[...] = jnp.zeros_like(acc_ref)
    carry_ref[...] = jnp.zeros_like(carry_ref)
    causal = (lax.broadcasted_iota(jnp.int32, (tq, tq), 1)
              < lax.broadcasted_iota(jnp.int32, (tq, tq), 0))
    mat_row = lax.broadcasted_iota(jnp.int32, (tq, tq + LANES), 0)
    mat_col = lax.broadcasted_iota(jnp.int32, (tq, tq + LANES), 1)
    suffix_mat = jnp.where((mat_row > mat_col) | (mat_col >= tq), 1.0, 0.0).astype(BF16)

    def block(kb, diagonal):
        ks = pl.multiple_of(kb * tq, tq)
        smallest = None
        for h in range(hp):
            cols = pl.ds(h * HEAD_DIM, HEAD_DIM)
            z = lax.dot_general(q_ref[:, cols], k_ref[pl.ds(ks, tq), cols],
                                (((1,), (1,)), ((), ())), preferred_element_type=F32) * scale
            spz = _softplus(z)
            sp = jnp.where(causal, spz, 0.0) if diagonal else spz
            sp_hi = sp.astype(BF16)
            sp_lo = (sp - sp_hi.astype(F32)).astype(BF16)
            sums = (jnp.dot(sp_hi, suffix_mat, preferred_element_type=F32)
                    + jnp.dot(sp_lo, suffix_mat, preferred_element_type=F32))
            carry = carry_ref[h]
            decay = sums[:, :tq] + jnp.concatenate([carry] * (tq // LANES), axis=1)
            a = jnp.exp(z - spz - decay)
            if diagonal:
                a = jnp.where(causal, a, 0.0)
            acc_ref[:, cols] += jnp.dot(a.astype(BF16), v_ref[pl.ds(ks, tq), cols],
                                        preferred_element_type=F32)
            new_carry = carry + sums[:, tq:]
            carry_ref[h] = new_carry
            least = jnp.min(new_carry)
            smallest = least if smallest is None else jnp.minimum(smallest, least)
        return smallest

    def cond(state):
        kb, min_carry = state
        return jnp.logical_and(kb >= 0, min_carry < SB_DECAY_STOP)

    def body(state):
        kb, _ = state
        return kb - 1, block(kb, False)

    lax.while_loop(cond, body, (qi - 1, block(qi, True)))
    o_ref[...] = acc_ref[...].astype(o_ref.dtype)


def _sb_attention(qkv, *, n_heads):
    s = qkv.shape[0]
    tq = _tile(s, 256)
    hp = next(c for c in (4, 2, 1) if n_heads % c == 0)
    ng = n_heads // hp
    gw = hp * HEAD_DIM
    nbytes = 2 * _nbytes((tq, gw), BF16) + 2 * _nbytes((s, gw), BF16)
    scratch = _nbytes((tq, gw), F32) + _nbytes((hp, tq, LANES), F32)
    return pl.pallas_call(
        functools.partial(_sb_attn_kernel, tq=tq, hp=hp, scale=HEAD_DIM ** -0.5),
        grid=(ng, s // tq),
        in_specs=[pl.BlockSpec((tq, gw), lambda g, i: (i, g)),
                  pl.BlockSpec((s, gw), lambda g, i, ng=ng: (0, ng + g)),
                  pl.BlockSpec((s, gw), lambda g, i, ng=ng: (0, 2 * ng + g))],
        out_specs=pl.BlockSpec((tq, gw), lambda g, i: (i, g)),
        out_shape=jax.ShapeDtypeStruct((s, n_heads * HEAD_DIM), BF16),
        scratch_shapes=[pltpu.VMEM((tq, gw), F32), pltpu.VMEM((hp, tq, LANES), F32)],
        compiler_params=_params(("parallel", "arbitrary"), nbytes,
                                scratch + 12 * hp * _nbytes((tq, 2 * tq), F32)),
        name="sb_attention",
    )(qkv, qkv, qkv)


def _pool_kernel(u_ref, halo_ref, w_ref, scale_ref, o_ref, ext_ref, *, bm, gd, halo):
    i = pl.program_id(0)
    ext_ref[pl.ds(0, halo), :] = jnp.where(i > 0, halo_ref[...], 0.0)
    ext_ref[pl.ds(halo, bm), :] = u_ref[...]
    pos = i * bm + lax.broadcasted_iota(jnp.int32, (bm, gd), 0)
    for g, win in enumerate(POOL_WINDOWS):
        cols = pl.ds(g * gd, gd)
        tok = ext_ref[pl.ds(halo, bm), cols]
        total = tok
        for back in range(1, win):
            total = total + ext_ref[pl.ds(halo - back, bm), cols]
        count = jnp.minimum(pos + 1, win).astype(F32)
        pooled = (total / count - tok).astype(BF16)
        mixed = jnp.dot(pooled, w_ref[g], preferred_element_type=F32)
        o_ref[:, cols] = (mixed * scale_ref[:, cols]).astype(o_ref.dtype)


def _multiscale_pool(u, w_groups, pool_scale):
    s, pw = u.shape
    n_groups, gd, _ = w_groups.shape
    assert n_groups == len(POOL_WINDOWS)
    halo = 16
    assert halo >= max(POOL_WINDOWS) and s % halo == 0
    bm = _tile(s, 512)
    hb = bm // halo
    nbytes = (_nbytes((bm, pw), F32) + _nbytes((halo, pw), F32) + _nbytes(w_groups.shape, BF16)
              + _nbytes((bm, pw), BF16))
    scratch = _nbytes((bm + halo, pw), F32)
    return pl.pallas_call(
        functools.partial(_pool_kernel, bm=bm, gd=gd, halo=halo),
        grid=(s // bm,),
        in_specs=[pl.BlockSpec((bm, pw), lambda i: (i, 0)),
                  pl.BlockSpec((halo, pw), lambda i, hb=hb: (jnp.maximum(i * hb - 1, 0), 0)),
                  pl.BlockSpec((n_groups, gd, gd), lambda i: (0, 0, 0)),
                  pl.BlockSpec((1, pw), lambda i: (0, 0))],
        out_specs=pl.BlockSpec((bm, pw), lambda i: (i, 0)),
        out_shape=jax.ShapeDtypeStruct((s, pw), BF16),
        scratch_shapes=[pltpu.VMEM((bm + halo, pw), F32)],
        compiler_params=_params(("parallel",), nbytes, scratch + 4 * _nbytes((bm, gd), F32)),
        name="multiscale_pool",
    )(u, u, w_groups, pool_scale.reshape(1, pw))


def _mem_block_kernel(x_ref, gq_ref, wq_ref, kv_ref, wo_ref, gf_ref, x2_ref, ht_ref, att_ref,
                      *, n_heads, scale):
    width = n_heads * HEAD_DIM
    x1 = x_ref[...]
    hq = _rmsnorm_rows(x1, gq_ref[...]).astype(BF16)
    q = jnp.dot(hq, wq_ref[...], preferred_element_type=F32).astype(BF16)
    for h in range(n_heads):
        lo = h * HEAD_DIM
        k = kv_ref[:, pl.ds(lo, HEAD_DIM)]
        v = kv_ref[:, pl.ds(width + lo, HEAD_DIM)]
        s = lax.dot_general(q[:, lo:lo + HEAD_DIM], k, (((1,), (1,)), ((), ())),
                            preferred_element_type=F32) * scale
        e = jnp.exp(s - jnp.max(s, axis=-1, keepdims=True))
        p = e / jnp.sum(e, axis=-1, keepdims=True)
        att_ref[:, pl.ds(lo, HEAD_DIM)] = jnp.dot(
            p.astype(BF16), v, preferred_element_type=F32).astype(att_ref.dtype)
    x2 = x1 + jnp.dot(att_ref[...], wo_ref[...], preferred_element_type=F32)
    x2_ref[...] = x2
    ht_ref[...] = _rmsnorm_rows(x2, gf_ref[...]).T.astype(ht_ref.dtype)


def _memory_block(x, kv, g_q, w_q, w_o, g_next, *, n_heads):
    s, d = x.shape
    m = kv.shape[0]
    width = n_heads * HEAD_DIM
    bm = _tile(s, 256)
    whole = lambda i: (0, 0)
    nbytes = (2 * _nbytes((bm, d), F32) + _nbytes((d, bm), BF16) + 2 * _nbytes((d, width), BF16)
              + _nbytes(kv.shape, BF16))
    scratch = _nbytes((bm, width), BF16) + 3 * _nbytes((bm, d), F32)
    return pl.pallas_call(
        functools.partial(_mem_block_kernel, n_heads=n_heads, scale=HEAD_DIM ** -0.5),
        grid=(s // bm,),
        in_specs=[pl.BlockSpec((bm, d), lambda i: (i, 0)),
                  pl.BlockSpec((1, d), whole),
                  pl.BlockSpec((d, width), whole),
                  pl.BlockSpec((m, 2 * width), whole),
                  pl.BlockSpec((width, d), whole),
                  pl.BlockSpec((1, d), whole)],
        out_specs=[pl.BlockSpec((bm, d), lambda i: (i, 0)),
                   pl.BlockSpec((d, bm), lambda i: (0, i))],
        out_shape=[jax.ShapeDtypeStruct((s, d), F32), jax.ShapeDtypeStruct((d, s), BF16)],
        scratch_shapes=[pltpu.VMEM((bm, width), BF16)],
        compiler_params=_params(("parallel",), nbytes, scratch),
        name="memory_block",
    )(x, g_q.reshape(1, d), w_q, kv, w_o, g_next.reshape(1, d))


def _topk_rows(score_list, k):
    def body(r, carry):
        out = []
        for work, vals, idxs in carry:
            n, w = work.shape
            rows = lax.broadcasted_iota(jnp.int32, (n, w), 0).astype(F32)
            krows = lax.broadcasted_iota(jnp.int32, (k, w), 0)
            top = jnp.max(work, axis=0, keepdims=True)
            idx = jnp.min(jnp.where(work == top, rows, float(n)), axis=0, keepdims=True)
            work = jnp.where(rows == idx, -jnp.inf, work)
            vals = jnp.where(krows == r, top, vals)
            idxs = jnp.where(krows == r, idx, idxs)
            out.append((work, vals, idxs))
        return tuple(out)

    init = tuple((s, jnp.zeros((k, s.shape[1]), F32), jnp.zeros((k, s.shape[1]), F32))
                 for s in score_list)
    return [(vals, idxs) for _, vals, idxs in lax.fori_loop(0, k, body, init)]


def _peer_select_kernel(q_ref, keys_ref, a_ref, nb_ref, b_ref, rb_ref, *, half, tn):
    k = PEER_TOPK
    hk = k // 2
    n_keys = a_ref.shape[0]

    def rank_keys(lanes):
        s1 = jnp.dot(keys_ref[0], q_ref[pl.ds(0, half), lanes], preferred_element_type=F32)
        s2 = jnp.dot(keys_ref[1], q_ref[pl.ds(half, half), lanes], preferred_element_type=F32)
        (v1, i1), (v2, i2) = _topk_rows([s1, s2], k)
        cand = jnp.concatenate(
            [v1[0:1, :] + v2] + [v1[a:a + 1, :] + v2[0:hk, :] for a in range(1, hk)]
            + [v1[hk:k, :] + v2[0:1, :]], axis=0)
        return s1, s2, v1, i1, v2, i2, cand

    groups = [pl.ds(cg * LANES, LANES) for cg in range(tn // LANES)]
    ranked = {}
    for cg, lanes in enumerate(groups):
        if cg % 2 == 0:
            pair = groups[cg:cg + 2]
            keyed = [rank_keys(g) for g in pair]
            tops = _topk_rows([kd[-1] for kd in keyed], k)
            for off, (kd, top) in enumerate(zip(keyed, tops)):
                ranked[cg + off] = kd + top
        s1, s2, v1, i1, v2, i2, cand, best, bpos = ranked.pop(cg)
        tau = best[k - 1:k, :]
        last = bpos[k - 1:k, :]
        pos = lax.broadcasted_iota(jnp.int32, cand.shape, 0).astype(F32)
        picked = jnp.where(cand > tau, 1.0,
                           jnp.where(cand == tau, jnp.where(pos <= last, 1.0, 0.0), 0.0))
        nb = [jnp.sum(picked[0:k, :], axis=0, keepdims=True)]
        nb += [jnp.sum(picked[k + (a - 1) * hk:k + a * hk, :], axis=0, keepdims=True)
               for a in range(1, hk)]
        tail = k + (hk - 1) * hk
        nb += [picked[tail + a:tail + a + 1, :] for a in range(k - hk)]
        rows = lax.broadcasted_iota(jnp.int32, (n_keys, LANES), 0).astype(F32)
        nb_dense = jnp.zeros((n_keys, LANES), F32)
        rb = jnp.full((n_keys, LANES), RANK_NONE, F32)
        for r in range(k):
            nb_dense = jnp.where(rows == i1[r:r + 1, :], nb[r], nb_dense)
            rb = jnp.where(rows == i2[r:r + 1, :], float(r), rb)
        z = jnp.sum(jnp.exp(best - best[0:1, :]), axis=0, keepdims=True)
        a_ref[:, lanes] = jnp.exp(s1 - v1[0:1, :])
        b_ref[:, lanes] = jnp.exp(s2 - v2[0:1, :]) / z
        nb_ref[:, lanes] = nb_dense
        rb_ref[:, lanes] = rb


def _peer_select(q_t, sub_keys):
    n_heads, _, n_keys, half = sub_keys.shape
    s = q_t.shape[1]
    assert PEER_TOPK % (2 * SUBLANES) == 0 and n_keys >= PEER_TOPK
    tn = _tile(s, 512)
    dense = jax.ShapeDtypeStruct((n_heads, n_keys, s), F32)
    dense_spec = pl.BlockSpec((None, n_keys, tn), lambda i, h: (h, 0, i))
    nbytes = (_nbytes((2 * half, tn), BF16) + _nbytes((2, n_keys, half), BF16)
              + 4 * _nbytes((n_keys, tn), F32))
    return pl.pallas_call(
        functools.partial(_peer_select_kernel, half=half, tn=tn),
        grid=(s // tn, n_heads),
        in_specs=[pl.BlockSpec((2 * half, tn), lambda i, h: (h, i)),
                  pl.BlockSpec((None, 2, n_keys, half), lambda i, h: (h, 0, 0, 0))],
        out_specs=[dense_spec] * 4,
        out_shape=[dense] * 4,
        compiler_params=_params(("parallel", "parallel"), nbytes, 8 * 2**20),
        name="peer_select",
    )(q_t, sub_keys)


def _gelu(x):
    return 0.5 * x * (1.0 + lax.erf(x * (2.0 ** -0.5)))


def _peer_dense_kernel(ht_ref, dn_ref, upt_ref, a_ref, nb_ref, b_ref, rb_ref, o_ref,
                       pt0_ref, pt1_ref, wt0_ref, wt1_ref,
                       *, n_heads, n_keys, n_tiles, te, tn, irows, jc, pieces):
    t = pl.program_id(0)
    n_pairs = pl.num_programs(0) - 2
    rps = te // n_keys
    gate_tile = jnp.clip(t - 1, 0, n_pairs - 1) % n_tiles
    ioff = (gate_tile * rps) % irows
    chunks_per_i = n_keys // jc

    @pl.when(t == 0)
    def _():
        pt1_ref[...] = jnp.zeros_like(pt1_ref)
        wt0_ref[...] = jnp.zeros_like(wt0_ref)

    @pl.when(jnp.logical_or(t == 0, (t - 2) % n_tiles == 0))
    def _():
        o_ref[...] = jnp.zeros_like(o_ref)

    def step(pt_new, pt_old, wt_new, wt_old):
        def gate_chunk(c):
            irow = ioff + c // chunks_per_i
            jrows = pl.ds((c % chunks_per_i) * jc, jc)
            gate = jnp.zeros((jc, tn), F32)
            for h in range(n_heads):
                prod = a_ref[h, pl.ds(irow, 1), :] * b_ref[h, jrows, :]
                picked = rb_ref[h, jrows, :] < nb_ref[h, pl.ds(irow, 1), :]
                gate = gate + jnp.where(picked, prod, 0.0)
            act = _gelu(pt_old[pl.ds(c * jc, jc), :])
            wt_new[pl.ds(c * jc, jc), :] = (gate * act).astype(wt_new.dtype)

        n_chunks = te // jc
        d_per = ht_ref.shape[0] // pieces
        for p in range(pieces):
            span = pl.ds(p * d_per, d_per)
            o_ref[span, :] += jnp.dot(upt_ref[span, :], wt_old[...], preferred_element_type=F32)
            for c in range(2 * p * n_chunks // (2 * pieces), (2 * p + 1) * n_chunks // (2 * pieces)):
                gate_chunk(c)
            part = jnp.dot(dn_ref[:, span], ht_ref[span, :], preferred_element_type=F32)
            if p == 0:
                pt_new[...] = part
            else:
                pt_new[...] += part
            for c in range((2 * p + 1) * n_chunks // (2 * pieces), (2 * p + 2) * n_chunks // (2 * pieces)):
                gate_chunk(c)

    @pl.when(t % 2 == 0)
    def _():
        step(pt0_ref, pt1_ref, wt1_ref, wt0_ref)

    @pl.when(t % 2 == 1)
    def _():
        step(pt1_ref, pt0_ref, wt0_ref, wt1_ref)


def _peer_dense(h_t, down, up_t, sel):
    a, nb, b, rb = sel
    n_heads, n_keys, s = a.shape
    d = h_t.shape[0]
    n_exp = down.shape[0]
    tn = _tile(s, 512)
    te = _tile(n_exp, 512)
    assert te % n_keys == 0
    n_tiles = n_exp // te
    rps = te // n_keys
    irows = max(rps, SUBLANES)
    assert irows % rps == 0 and n_keys % irows == 0
    jc = 16
    pieces = next(p for p in (16, 8, 4, 2, 1)
                  if d % (p * LANES) == 0 and (te // jc) % (2 * p) == 0)
    n_pairs = (s // tn) * n_tiles
    pair = lambda t, lag: jnp.clip(t - lag, 0, n_pairs - 1)
    tok = lambda t, lag: pair(t, lag) // n_tiles
    exp = lambda t, lag: pair(t, lag) % n_tiles
    once = pl.Buffered(1)
    by_i = pl.BlockSpec((n_heads, irows, tn), lambda t: (0, (exp(t, 1) * rps) // irows, tok(t, 1)))
    by_j = pl.BlockSpec((n_heads, n_keys, tn), lambda t: (0, 0, tok(t, 1)), pipeline_mode=once)
    nbytes = (2 * _nbytes((te, d), BF16) + 2 * _nbytes((n_heads, irows, tn), F32)
              + _nbytes((d, tn), F32))
    scratch = (2 * _nbytes((te, tn), F32) + 2 * _nbytes((te, tn), BF16) + _nbytes((d, tn), BF16)
               + 2 * _nbytes((n_heads, n_keys, tn), F32))
    return pl.pallas_call(
        functools.partial(_peer_dense_kernel, n_heads=n_heads, n_keys=n_keys, n_tiles=n_tiles,
                          te=te, tn=tn, irows=irows, jc=jc, pieces=pieces),
        grid=(n_pairs + 2,),
        in_specs=[pl.BlockSpec((d, tn), lambda t: (0, tok(t, 0)), pipeline_mode=once),
                  pl.BlockSpec((te, d), lambda t: (exp(t, 0), 0)),
                  pl.BlockSpec((d, te), lambda t: (0, exp(t, 2))),
                  by_i, by_i, by_j, by_j],
        out_specs=pl.BlockSpec((d, tn), lambda t: (0, tok(t, 2))),
        out_shape=jax.ShapeDtypeStruct((d, s), F32),
        scratch_shapes=[pltpu.VMEM((te, tn), F32), pltpu.VMEM((te, tn), F32),
                        pltpu.VMEM((te, tn), BF16), pltpu.VMEM((te, tn), BF16)],
        compiler_params=_params(("arbitrary",), nbytes, scratch),
        name="peer_dense",
    )(h_t, down, up_t, a, nb, b, rb)


def _add_t_kernel(x_ref, yt_ref, g_ref, o_ref, *, normalize):
    x = x_ref[...] + yt_ref[...].T
    if normalize:
        x = _rmsnorm_rows(x, g_ref[...])
    o_ref[...] = x


def _add_transposed(x, y_t, g, *, normalize):
    s, d = x.shape
    bm = _tile(s, 256)
    nbytes = 3 * _nbytes((bm, d), F32)
    return pl.pallas_call(
        functools.partial(_add_t_kernel, normalize=normalize),
        grid=(s // bm,),
        in_specs=[pl.BlockSpec((bm, d), lambda i: (i, 0)),
                  pl.BlockSpec((d, bm), lambda i: (0, i)),
                  pl.BlockSpec((1, d), lambda i: (0, 0))],
        out_specs=pl.BlockSpec((bm, d), lambda i: (i, 0)),
        out_shape=jax.ShapeDtypeStruct((s, d), F32),
        compiler_params=_params(("parallel",), nbytes, 2 * _nbytes((bm, d), F32)),
        name="add_transposed",
    )(x, y_t, g.reshape(1, d))


def _layer(x, mem, p, *, last, norm_final):
    d = x.shape[1]
    sb_w = p["w_branch_sb"].shape[0]
    pool_w = p["w_branch_pool"].shape[0]
    n_sb_heads = sb_w // HEAD_DIM
    xa_w = p["xa_w_q"].shape[1]
    bf = lambda w: w.astype(BF16)

    w_in = bf(p["w_in"])
    h1 = _rmsnorm(x, p["norm_mix"], out_dtype=BF16)
    qkv = _matmul(h1, w_in, out_dtype=BF16, n_cols=3 * sb_w)
    u_pool = _matmul(h1, w_in, out_dtype=F32, col_start=3 * sb_w, n_cols=pool_w)
    gates = _matmul(h1, w_in, out_dtype=F32, epilogue="sigmoid", col_start=3 * sb_w + pool_w)
    attn = _sb_attention(qkv, n_heads=n_sb_heads)
    pooled = _multiscale_pool(u_pool, bf(p["pool_group_w"]), p["pool_scale"])
    mix = _gated_mix(attn, pooled, bf(p["w_branch_sb"]), bf(p["w_branch_pool"]), gates, d_model=d)
    x1 = _matmul(mix, bf(p["w_out"]), out_dtype=F32, epilogue="residual", residual=x)

    mem_n = _rmsnorm(mem, p["norm_mem_kv"], out_dtype=BF16)
    xkv = _matmul(mem_n, bf(p["xa_w_kv"]), out_dtype=BF16)
    x2, h3_t = _memory_block(x1, xkv, p["norm_mem_q"], bf(p["xa_w_q"]), bf(p["xa_w_o"]),
                             p["norm_ffn"], n_heads=xa_w // HEAD_DIM)

    q_t = _matmul(bf(p["peer_w_query"].T), h3_t, out_dtype=BF16)
    sel = _peer_select(q_t, bf(p["peer_sub_keys"]))
    y_t = _peer_dense(h3_t, bf(p["peer_down"]), bf(p["peer_up"].T), sel)
    return _add_transposed(x2, y_t, norm_final, normalize=last)


_LAYER_PARAMS = ("norm_mix", "norm_mem_q", "norm_mem_kv", "norm_ffn", "w_in", "pool_group_w",
                 "pool_scale", "w_branch_sb", "w_branch_pool", "w_out", "xa_w_q", "xa_w_kv",
                 "xa_w_o", "peer_w_query", "peer_sub_keys", "peer_down", "peer_up")


def kernel(x, mem, norm_mix, norm_mem_q, norm_mem_kv, norm_ffn, norm_final, w_in, pool_group_w,
           pool_scale, w_branch_sb, w_branch_pool, w_out, xa_w_q, xa_w_kv, xa_w_o,
           peer_w_query, peer_sub_keys, peer_down, peer_up):
    stacked = dict(norm_mix=norm_mix, norm_mem_q=norm_mem_q, norm_mem_kv=norm_mem_kv,
                   norm_ffn=norm_ffn, w_in=w_in, pool_group_w=pool_group_w, pool_scale=pool_scale,
                   w_branch_sb=w_branch_sb, w_branch_pool=w_branch_pool, w_out=w_out,
                   xa_w_q=xa_w_q, xa_w_kv=xa_w_kv, xa_w_o=xa_w_o, peer_w_query=peer_w_query,
                   peer_sub_keys=peer_sub_keys, peer_down=peer_down, peer_up=peer_up)
    depth = w_in.shape[0]
    outs = []
    for b in range(x.shape[0]):
        xb = x[b]
        for l in range(depth):
            p = {name: stacked[name][l] for name in _LAYER_PARAMS}
            xb = _layer(xb, mem[b], p, last=(l == depth - 1), norm_final=norm_final)
        outs.append(xb)
    return jnp.stack(outs, axis=0)
```

```python
import functools
import math

import jax
import jax.numpy as jnp
from jax import lax
from jax.experimental import pallas as pl
from jax.experimental.pallas import tpu as pltpu

F32 = jnp.float32
BF16 = jnp.bfloat16

HEAD_DIM = 128
POOL_WINDOWS = (2, 4, 8, 16)
PEER_TOPK = 16
EPS = 1e-6

V7X_VMEM_BYTES = 64 * 2**20
V7X_VMEM_BUDGET = 56 * 2**20
LANES = 128
SUBLANES = 8

SB_DECAY_STOP = 120.0
RANK_NONE = 1.0e9


def _vmem_limit(pipelined_bytes, scratch_bytes=0):
    need = 2 * pipelined_bytes + scratch_bytes + 12 * 2**20
    return int(min(V7X_VMEM_BUDGET, max(need, 16 * 2**20)))


def _nbytes(shape, dtype):
    return math.prod(shape) * jnp.dtype(dtype).itemsize


def _params(semantics, pipelined_bytes, scratch_bytes=0):
    return pltpu.CompilerParams(
        dimension_semantics=semantics,
        vmem_limit_bytes=_vmem_limit(pipelined_bytes, scratch_bytes))


def _tile(n, pref):
    t = min(n, pref)
    assert n % t == 0, (n, pref)
    return t


def _rmsnorm_rows(x, g):
    return x * lax.rsqrt(jnp.mean(x * x, axis=-1, keepdims=True) + EPS) * g


def _rmsnorm_kernel(x_ref, g_ref, o_ref, *, transpose):
    y = _rmsnorm_rows(x_ref[...], g_ref[...])
    if transpose:
        y = y.T
    o_ref[...] = y.astype(o_ref.dtype)


def _rmsnorm(x, g, *, out_dtype, transpose=False):
    m, d = x.shape
    bm = _tile(m, 256)
    if transpose:
        out_shape = jax.ShapeDtypeStruct((d, m), out_dtype)
        out_spec = pl.BlockSpec((d, bm), lambda i: (0, i))
    else:
        out_shape = jax.ShapeDtypeStruct((m, d), out_dtype)
        out_spec = pl.BlockSpec((bm, d), lambda i: (i, 0))
    nbytes = _nbytes((bm, d), F32) + _nbytes((bm, d), out_dtype)
    return pl.pallas_call(
        functools.partial(_rmsnorm_kernel, transpose=transpose),
        grid=(m // bm,),
        in_specs=[pl.BlockSpec((bm, d), lambda i: (i, 0)),
                  pl.BlockSpec((1, d), lambda i: (0, 0))],
        out_specs=out_spec,
        out_shape=out_shape,
        compiler_params=_params(("parallel",), nbytes, 2 * _nbytes((bm, d), F32)),
        name="rmsnorm",
    )(x, g.reshape(1, d))


def _mm_kernel(a_ref, w_ref, *rest, epilogue):
    o_ref = rest[-1]
    acc = jnp.dot(a_ref[...], w_ref[...], preferred_element_type=F32)
    if epilogue == "sigmoid":
        acc = jax.nn.sigmoid(acc)
    elif epilogue == "residual":
        acc = rest[0][...] + acc
    o_ref[...] = acc.astype(o_ref.dtype)


def _matmul(a, w, *, out_dtype, epilogue="none", residual=None, bm=1024, bn=1024,
            col_start=0, n_cols=None):
    m, k = a.shape
    k2, n_all = w.shape
    assert k == k2
    n = n_all - col_start if n_cols is None else n_cols
    bm = _tile(m, bm)
    bn = next(t for t in (bn, bn // 2, bn // 4, LANES) if n % t == 0 and col_start % t == 0)
    j0 = col_start // bn
    in_specs = [pl.BlockSpec((bm, k), lambda i, j: (i, 0)),
                pl.BlockSpec((k, bn), lambda i, j: (0, j0 + j))]
    args = [a, w]
    nbytes = _nbytes((bm, k), a.dtype) + _nbytes((k, bn), w.dtype) + _nbytes((bm, bn), out_dtype)
    if epilogue == "residual":
        in_specs.append(pl.BlockSpec((bm, bn), lambda i, j: (i, j)))
        args.append(residual)
        nbytes += _nbytes((bm, bn), residual.dtype)
    return pl.pallas_call(
        functools.partial(_mm_kernel, epilogue=epilogue),
        grid=(m // bm, n // bn),
        in_specs=in_specs,
        out_specs=pl.BlockSpec((bm, bn), lambda i, j: (i, j)),
        out_shape=jax.ShapeDtypeStruct((m, n), out_dtype),
        compiler_params=_params(("parallel", "parallel"), nbytes, _nbytes((bm, bn), F32)),
        name="matmul_" + epilogue,
    )(*args)


def _mix_kernel(a1_ref, a2_ref, w1_ref, w2_ref, g1_ref, g2_ref, o_ref):
    y1 = jnp.dot(a1_ref[...], w1_ref[...], preferred_element_type=F32)
    y2 = jnp.dot(a2_ref[...], w2_ref[...], preferred_element_type=F32)
    o_ref[...] = (g1_ref[...] * y1 + g2_ref[...] * y2).astype(o_ref.dtype)


def _gated_mix(attn, pooled, w_sb, w_pool, gates, *, d_model):
    s, k1 = attn.shape
    k2 = pooled.shape[1]
    bm = _tile(s, 1024)
    bn = _tile(d_model, 512)
    nj = d_model // bn
    nbytes = (_nbytes((bm, k1), BF16) + _nbytes((bm, k2), BF16) + _nbytes((k1, bn), BF16)
              + _nbytes((k2, bn), BF16) + 2 * _nbytes((bm, bn), F32) + _nbytes((bm, bn), BF16))
    return pl.pallas_call(
        _mix_kernel,
        grid=(s // bm, nj),
        in_specs=[pl.BlockSpec((bm, k1), lambda i, j: (i, 0)),
                  pl.BlockSpec((bm, k2), lambda i, j: (i, 0)),
                  pl.BlockSpec((k1, bn), lambda i, j: (0, j)),
                  pl.BlockSpec((k2, bn), lambda i, j: (0, j)),
                  pl.BlockSpec((bm, bn), lambda i, j: (i, j)),
                  pl.BlockSpec((bm, bn), lambda i, j, nj=nj: (i, j + nj))],
        out_specs=pl.BlockSpec((bm, bn), lambda i, j: (i, j)),
        out_shape=jax.ShapeDtypeStruct((s, d_model), BF16),
        compiler_params=_params(("parallel", "parallel"), nbytes, 3 * _nbytes((bm, bn), F32)),
        name="gated_mix",
    )(attn, pooled, w_sb, w_pool, gates, gates)


def _softplus(z):
    return jnp.maximum(z, 0.0) + jnp.log(1.0 + jnp.exp(-jnp.abs(z)))


def _sb_attn_kernel(q_ref, k_ref, v_ref, o_ref, acc_ref, carry_ref, *, tq, hp, scale):
    qi = pl.program_id(1)
    acc_ref[...] = jnp.zeros_like(acc_ref)
    carry_ref[...] = jnp.zeros_like(carry_ref)
    causal = (lax.broadcasted_iota(jnp.int32, (tq, tq), 1)
              < lax.broadcasted_iota(jnp.int32, (tq, tq), 0))
    suffix_mat = jnp.where(lax.broadcasted_iota(jnp.int32, (tq, tq), 0)
                           > lax.broadcasted_iota(jnp.int32, (tq, tq), 1), 1.0, 0.0).astype(BF16)

    def block(kb, diagonal):
        ks = pl.multiple_of(kb * tq, tq)
        heads = range(hp)
        cols = [pl.ds(h * HEAD_DIM, HEAD_DIM) for h in heads]
        z = [lax.dot_general(q_ref[:, cols[h]], k_ref[pl.ds(ks, tq), cols[h]],
                             (((1,), (1,)), ((), ())), preferred_element_type=F32) * scale
             for h in heads]
        spz = [_softplus(z[h]) for h in heads]
        sp = [jnp.where(causal, spz[h], 0.0) if diagonal else spz[h] for h in heads]
        sp_hi = [sp[h].astype(BF16) for h in heads]
        sp_lo = [(sp[h] - sp_hi[h].astype(F32)).astype(BF16) for h in heads]
        sums = [jnp.dot(sp_hi[h], suffix_mat, preferred_element_type=F32)
                + jnp.dot(sp_lo[h], suffix_mat, preferred_element_type=F32) for h in heads]
        smallest = None
        for h in heads:
            carry = carry_ref[h]
            decay = sums[h] + jnp.concatenate([carry] * (tq // LANES), axis=1)
            a = jnp.exp(z[h] - spz[h] - decay)
            if diagonal:
                a = jnp.where(causal, a, 0.0)
            acc_ref[:, cols[h]] += jnp.dot(a.astype(BF16), v_ref[pl.ds(ks, tq), cols[h]],
                                           preferred_element_type=F32)
            new_carry = carry + jnp.broadcast_to(sums[h][:, 0:1] + sp[h][:, 0:1], carry.shape)
            carry_ref[h] = new_carry
            least = jnp.min(new_carry)
            smallest = least if smallest is None else jnp.minimum(smallest, least)
        return smallest

    def cond(state):
        kb, min_carry = state
        return jnp.logical_and(kb >= 0, min_carry < SB_DECAY_STOP)

    def body(state):
        kb, _ = state
        return kb - 1, block(kb, False)

    lax.while_loop(cond, body, (qi - 1, block(qi, True)))
    o_ref[...] = acc_ref[...].astype(o_ref.dtype)


def _sb_attention(qkv, *, n_heads):
    s = qkv.shape[0]
    tq = _tile(s, 256)
    hp = next(c for c in (4, 2, 1) if n_heads % c == 0)
    ng = n_heads // hp
    gw = hp * HEAD_DIM
    nbytes = 2 * _nbytes((tq, gw), BF16) + 2 * _nbytes((s, gw), BF16)
    scratch = _nbytes((tq, gw), F32) + _nbytes((hp, tq, LANES), F32)
    return pl.pallas_call(
        functools.partial(_sb_attn_kernel, tq=tq, hp=hp, scale=HEAD_DIM ** -0.5),
        grid=(ng, s // tq),
        in_specs=[pl.BlockSpec((tq, gw), lambda g, i: (i, g)),
                  pl.BlockSpec((s, gw), lambda g, i, ng=ng: (0, ng + g)),
                  pl.BlockSpec((s, gw), lambda g, i, ng=ng: (0, 2 * ng + g))],
        out_specs=pl.BlockSpec((tq, gw), lambda g, i: (i, g)),
        out_shape=jax.ShapeDtypeStruct((s, n_heads * HEAD_DIM), BF16),
        scratch_shapes=[pltpu.VMEM((tq, gw), F32), pltpu.VMEM((hp, tq, LANES), F32)],
        compiler_params=_params(("parallel", "arbitrary"), nbytes,
                                scratch + 12 * hp * _nbytes((tq, 2 * tq), F32)),
        name="sb_attention",
    )(qkv, qkv, qkv)


def _pool_kernel(u_ref, halo_ref, w_ref, scale_ref, o_ref, ext_ref, *, bm, gd, halo):
    i = pl.program_id(0)
    ext_ref[pl.ds(0, halo), :] = jnp.where(i > 0, halo_ref[...], 0.0)
    ext_ref[pl.ds(halo, bm), :] = u_ref[...]
    pos = i * bm + lax.broadcasted_iota(jnp.int32, (bm, gd), 0)
    for g, win in enumerate(POOL_WINDOWS):
        cols = pl.ds(g * gd, gd)
        tok = ext_ref[pl.ds(halo, bm), cols]
        total = tok
        for back in range(1, win):
            total = total + ext_ref[pl.ds(halo - back, bm), cols]
        count = jnp.minimum(pos + 1, win).astype(F32)
        pooled = (total / count - tok).astype(BF16)
        mixed = jnp.dot(pooled, w_ref[g], preferred_element_type=F32)
        o_ref[:, cols] = (mixed * scale_ref[:, cols]).astype(o_ref.dtype)


def _multiscale_pool(u, w_groups, pool_scale):
    s, pw = u.shape
    n_groups, gd, _ = w_groups.shape
    assert n_groups == len(POOL_WINDOWS)
    halo = 16
    assert halo >= max(POOL_WINDOWS) and s % halo == 0
    bm = _tile(s, 512)
    hb = bm // halo
    nbytes = (_nbytes((bm, pw), F32) + _nbytes((halo, pw), F32) + _nbytes(w_groups.shape, BF16)
              + _nbytes((bm, pw), BF16))
    scratch = _nbytes((bm + halo, pw), F32)
    return pl.pallas_call(
        functools.partial(_pool_kernel, bm=bm, gd=gd, halo=halo),
        grid=(s // bm,),
        in_specs=[pl.BlockSpec((bm, pw), lambda i: (i, 0)),
                  pl.BlockSpec((halo, pw), lambda i, hb=hb: (jnp.maximum(i * hb - 1, 0), 0)),
                  pl.BlockSpec((n_groups, gd, gd), lambda i: (0, 0, 0)),
                  pl.BlockSpec((1, pw), lambda i: (0, 0))],
        out_specs=pl.BlockSpec((bm, pw), lambda i: (i, 0)),
        out_shape=jax.ShapeDtypeStruct((s, pw), BF16),
        scratch_shapes=[pltpu.VMEM((bm + halo, pw), F32)],
        compiler_params=_params(("parallel",), nbytes, scratch + 4 * _nbytes((bm, gd), F32)),
        name="multiscale_pool",
    )(u, u, w_groups, pool_scale.reshape(1, pw))


def _mem_block_kernel(x_ref, gq_ref, wq_ref, kv_ref, wo_ref, gf_ref, x2_ref, ht_ref, att_ref,
                      *, n_heads, scale):
    width = n_heads * HEAD_DIM
    x1 = x_ref[...]
    hq = _rmsnorm_rows(x1, gq_ref[...]).astype(BF16)
    q = jnp.dot(hq, wq_ref[...], preferred_element_type=F32).astype(BF16)
    for h in range(n_heads):
        lo = h * HEAD_DIM
        k = kv_ref[:, pl.ds(lo, HEAD_DIM)]
        v = kv_ref[:, pl.ds(width + lo, HEAD_DIM)]
        s = lax.dot_general(q[:, lo:lo + HEAD_DIM], k, (((1,), (1,)), ((), ())),
                            preferred_element_type=F32) * scale
        e = jnp.exp(s - jnp.max(s, axis=-1, keepdims=True))
        p = e / jnp.sum(e, axis=-1, keepdims=True)
        att_ref[:, pl.ds(lo, HEAD_DIM)] = jnp.dot(
            p.astype(BF16), v, preferred_element_type=F32).astype(att_ref.dtype)
    x2 = x1 + jnp.dot(att_ref[...], wo_ref[...], preferred_element_type=F32)
    x2_ref[...] = x2
    ht_ref[...] = _rmsnorm_rows(x2, gf_ref[...]).T.astype(ht_ref.dtype)


def _memory_block(x, kv, g_q, w_q, w_o, g_next, *, n_heads):
    s, d = x.shape
    m = kv.shape[0]
    width = n_heads * HEAD_DIM
    bm = _tile(s, 256)
    whole = lambda i: (0, 0)
    nbytes = (2 * _nbytes((bm, d), F32) + _nbytes((d, bm), BF16) + 2 * _nbytes((d, width), BF16)
              + _nbytes(kv.shape, BF16))
    scratch = _nbytes((bm, width), BF16) + 3 * _nbytes((bm, d), F32)
    return pl.pallas_call(
        functools.partial(_mem_block_kernel, n_heads=n_heads, scale=HEAD_DIM ** -0.5),
        grid=(s // bm,),
        in_specs=[pl.BlockSpec((bm, d), lambda i: (i, 0)),
                  pl.BlockSpec((1, d), whole),
                  pl.BlockSpec((d, width), whole),
                  pl.BlockSpec((m, 2 * width), whole),
                  pl.BlockSpec((width, d), whole),
                  pl.BlockSpec((1, d), whole)],
        out_specs=[pl.BlockSpec((bm, d), lambda i: (i, 0)),
                   pl.BlockSpec((d, bm), lambda i: (0, i))],
        out_shape=[jax.ShapeDtypeStruct((s, d), F32), jax.ShapeDtypeStruct((d, s), BF16)],
        scratch_shapes=[pltpu.VMEM((bm, width), BF16)],
        compiler_params=_params(("parallel",), nbytes, scratch),
        name="memory_block",
    )(x, g_q.reshape(1, d), w_q, kv, w_o, g_next.reshape(1, d))


def _topk_rows(score_list, k):
    row_ids = {s.shape: lax.broadcasted_iota(jnp.int32, s.shape, 0).astype(F32) for s in score_list}

    def body(r, carry):
        out = []
        for work, vals, idxs in carry:
            n, w = work.shape
            krows = lax.broadcasted_iota(jnp.int32, (k, w), 0)
            rid = row_ids[work.shape]
            parts = [(work[g:g + SUBLANES, :], rid[g:g + SUBLANES, :]) for g in range(0, n, SUBLANES)]
            while len(parts) > 1:
                merged = []
                for left in range(0, len(parts) - 1, 2):
                    (va, ia), (vb, ib) = parts[left], parts[left + 1]
                    merged.append((jnp.maximum(va, vb), jnp.where(va >= vb, ia, ib)))
                if len(parts) % 2:
                    merged.append(parts[-1])
                parts = merged
            v8, i8 = parts[0]
            top = jnp.max(v8, axis=0, keepdims=True)
            idx = jnp.min(jnp.where(v8 == top, i8, float(n)), axis=0, keepdims=True)
            work = jnp.where(rid == idx, -jnp.inf, work)
            vals = jnp.where(krows == r, top, vals)
            idxs = jnp.where(krows == r, idx, idxs)
            out.append((work, vals, idxs))
        return tuple(out)

    init = tuple((s, jnp.zeros((k, s.shape[1]), F32), jnp.zeros((k, s.shape[1]), F32))
                 for s in score_list)
    return [(vals, idxs) for _, vals, idxs in lax.fori_loop(0, k, body, init)]


def _peer_select_kernel(q_ref, keys_ref, a_ref, nb_ref, b_ref, rb_ref, *, half, tn):
    k = PEER_TOPK
    hk = k // 2
    n_keys = a_ref.shape[0]

    def rank_keys(lanes):
        s1 = jnp.dot(keys_ref[0], q_ref[pl.ds(0, half), lanes], preferred_element_type=F32)
        s2 = jnp.dot(keys_ref[1], q_ref[pl.ds(half, half), lanes], preferred_element_type=F32)
        (v1, i1), (v2, i2) = _topk_rows([s1, s2], k)
        cand = jnp.concatenate(
            [v1[0:1, :] + v2] + [v1[a:a + 1, :] + v2[0:hk, :] for a in range(1, hk)]
            + [v1[hk:k, :] + v2[0:1, :]], axis=0)
        return s1, s2, v1, i1, v2, i2, cand

    groups = [pl.ds(cg * LANES, LANES) for cg in range(tn // LANES)]
    ranked = {}
    for cg, lanes in enumerate(groups):
        if cg % 2 == 0:
            pair = groups[cg:cg + 2]
            keyed = [rank_keys(g) for g in pair]
            tops = _topk_rows([kd[-1] for kd in keyed], k)
            for off, (kd, top) in enumerate(zip(keyed, tops)):
                ranked[cg + off] = kd + top
        s1, s2, v1, i1, v2, i2, cand, best, bpos = ranked.pop(cg)
        tau = best[k - 1:k, :]
        last = bpos[k - 1:k, :]
        pos = lax.broadcasted_iota(jnp.int32, cand.shape, 0).astype(F32)
        picked = jnp.where(cand > tau, 1.0,
                           jnp.where(cand == tau, jnp.where(pos <= last, 1.0, 0.0), 0.0))
        nb = [jnp.sum(picked[0:k, :], axis=0, keepdims=True)]
        nb += [jnp.sum(picked[k + (a - 1) * hk:k + a * hk, :], axis=0, keepdims=True)
               for a in range(1, hk)]
        tail = k + (hk - 1) * hk
        nb += [picked[tail + a:tail + a + 1, :] for a in range(k - hk)]
        rows = lax.broadcasted_iota(jnp.int32, (n_keys, LANES), 0).astype(F32)
        nb_dense = jnp.zeros((n_keys, LANES), F32)
        rb = jnp.full((n_keys, LANES), RANK_NONE, F32)
        for r in range(k):
            nb_dense = jnp.where(rows == i1[r:r + 1, :], nb[r], nb_dense)
            rb = jnp.where(rows == i2[r:r + 1, :], float(r), rb)
        z = jnp.sum(jnp.exp(best - best[0:1, :]), axis=0, keepdims=True)
        a_ref[:, lanes] = jnp.exp(s1 - v1[0:1, :])
        b_ref[:, lanes] = jnp.exp(s2 - v2[0:1, :]) / z
        nb_ref[:, lanes] = nb_dense
        rb_ref[:, lanes] = rb


def _peer_select(q_t, sub_keys):
    n_heads, _, n_keys, half = sub_keys.shape
    s = q_t.shape[1]
    assert PEER_TOPK % (2 * SUBLANES) == 0 and n_keys >= PEER_TOPK
    tn = _tile(s, 512)
    dense = jax.ShapeDtypeStruct((n_heads, n_keys, s), F32)
    dense_spec = pl.BlockSpec((None, n_keys, tn), lambda i, h: (h, 0, i))
    nbytes = (_nbytes((2 * half, tn), BF16) + _nbytes((2, n_keys, half), BF16)
              + 4 * _nbytes((n_keys, tn), F32))
    return pl.pallas_call(
        functools.partial(_peer_select_kernel, half=half, tn=tn),
        grid=(s // tn, n_heads),
        in_specs=[pl.BlockSpec((2 * half, tn), lambda i, h: (h, i)),
                  pl.BlockSpec((None, 2, n_keys, half), lambda i, h: (h, 0, 0, 0))],
        out_specs=[dense_spec] * 4,
        out_shape=[dense] * 4,
        compiler_params=_params(("parallel", "parallel"), nbytes, 8 * 2**20),
        name="peer_select",
    )(q_t, sub_keys)


def _gelu(x):
    return 0.5 * x * (1.0 + lax.erf(x * (2.0 ** -0.5)))


def _peer_dense_kernel(ht_ref, dn_ref, upt_ref, a_ref, nb_ref, b_ref, rb_ref, o_ref,
                       pt0_ref, pt1_ref, wt0_ref, wt1_ref,
                       *, n_heads, n_keys, n_tiles, te, tn, irows, jc, pieces):
    t = pl.program_id(0)
    n_pairs = pl.num_programs(0) - 2
    rps = te // n_keys
    gate_tile = jnp.clip(t - 1, 0, n_pairs - 1) % n_tiles
    ioff = (gate_tile * rps) % irows
    chunks_per_i = n_keys // jc

    @pl.when(t == 0)
    def _():
        pt1_ref[...] = jnp.zeros_like(pt1_ref)
        wt0_ref[...] = jnp.zeros_like(wt0_ref)

    @pl.when(jnp.logical_or(t == 0, (t - 2) % n_tiles == 0))
    def _():
        o_ref[...] = jnp.zeros_like(o_ref)

    def step(pt_new, pt_old, wt_new, wt_old):
        def gate_chunk(c):
            irow = ioff + c // chunks_per_i
            jrows = pl.ds((c % chunks_per_i) * jc, jc)
            gate = jnp.zeros((jc, tn), F32)
            for h in range(n_heads):
                prod = a_ref[h, pl.ds(irow, 1), :] * b_ref[h, jrows, :]
                picked = rb_ref[h, jrows, :] < nb_ref[h, pl.ds(irow, 1), :]
                gate = gate + jnp.where(picked, prod, 0.0)
            act = _gelu(pt_old[pl.ds(c * jc, jc), :])
            wt_new[pl.ds(c * jc, jc), :] = (gate * act).astype(wt_new.dtype)

        n_chunks = te // jc
        d_per = ht_ref.shape[0] // pieces
        for p in range(pieces):
            span = pl.ds(p * d_per, d_per)
            o_ref[span, :] += jnp.dot(upt_ref[span, :], wt_old[...], preferred_element_type=F32)
            for c in range(2 * p * n_chunks // (2 * pieces), (2 * p + 1) * n_chunks // (2 * pieces)):
                gate_chunk(c)
            part = jnp.dot(dn_ref[:, span], ht_ref[span, :], preferred_element_type=F32)
            if p == 0:
                pt_new[...] = part
            else:
                pt_new[...] += part
            for c in range((2 * p + 1) * n_chunks // (2 * pieces), (2 * p + 2) * n_chunks // (2 * pieces)):
                gate_chunk(c)

    @pl.when(t % 2 == 0)
    def _():
        step(pt0_ref, pt1_ref, wt1_ref, wt0_ref)

    @pl.when(t % 2 == 1)
    def _():
        step(pt1_ref, pt0_ref, wt0_ref, wt1_ref)


def _peer_dense(h_t, down, up, sel):
    a, nb, b, rb = sel
    n_heads, n_keys, s = a.shape
    d = h_t.shape[0]
    n_exp = down.shape[0]
    tn = _tile(s, 512)
    te = _tile(n_exp, 512)
    assert te % n_keys == 0
    n_tiles = n_exp // te
    rps = te // n_keys
    irows = max(rps, SUBLANES)
    assert irows % rps == 0 and n_keys % irows == 0
    jc = 16
    pieces = next(p for p in (16, 8, 4, 2, 1)
                  if d % (p * LANES) == 0 and (te // jc) % (2 * p) == 0)
    up_t = jnp.transpose(up.reshape(n_tiles, te, d), (0, 2, 1)).astype(BF16)
    n_pairs = (s // tn) * n_tiles
    pair = lambda t, lag: jnp.clip(t - lag, 0, n_pairs - 1)
    tok = lambda t, lag: pair(t, lag) // n_tiles
    exp = lambda t, lag: pair(t, lag) % n_tiles
    once = pl.Buffered(1)
    by_i = pl.BlockSpec((n_heads, irows, tn), lambda t: (0, (exp(t, 1) * rps) // irows, tok(t, 1)))
    by_j = pl.BlockSpec((n_heads, n_keys, tn), lambda t: (0, 0, tok(t, 1)), pipeline_mode=once)
    nbytes = (2 * _nbytes((te, d), BF16) + 2 * _nbytes((n_heads, irows, tn), F32)
              + _nbytes((d, tn), F32) + _nbytes((d, tn), BF16))
    scratch = (2 * _nbytes((te, tn), F32) + 2 * _nbytes((te, tn), BF16)
               + 2 * _nbytes((n_heads, n_keys, tn), F32))
    return pl.pallas_call(
        functools.partial(_peer_dense_kernel, n_heads=n_heads, n_keys=n_keys, n_tiles=n_tiles,
                          te=te, tn=tn, irows=irows, jc=jc, pieces=pieces),
        grid=(n_pairs + 2,),
        in_specs=[pl.BlockSpec((d, tn), lambda t: (0, tok(t, 0))),
                  pl.BlockSpec((te, d), lambda t: (exp(t, 0), 0)),
                  pl.BlockSpec((None, d, te), lambda t: (exp(t, 2), 0, 0)),
                  by_i, by_i, by_j, by_j],
        out_specs=pl.BlockSpec((d, tn), lambda t: (0, tok(t, 2))),
        out_shape=jax.ShapeDtypeStruct((d, s), F32),
        scratch_shapes=[pltpu.VMEM((te, tn), F32), pltpu.VMEM((te, tn), F32),
                        pltpu.VMEM((te, tn), BF16), pltpu.VMEM((te, tn), BF16)],
        compiler_params=_params(("arbitrary",), nbytes, scratch),
        name="peer_dense",
    )(h_t, down, up_t, a, nb, b, rb)


def _add_t_kernel(x_ref, yt_ref, g_ref, o_ref, *, normalize):
    x = x_ref[...] + yt_ref[...].T
    if normalize:
        x = _rmsnorm_rows(x, g_ref[...])
    o_ref[...] = x


def _add_transposed(x, y_t, g, *, normalize):
    s, d = x.shape
    bm = _tile(s, 256)
    nbytes = 3 * _nbytes((bm, d), F32)
    return pl.pallas_call(
        functools.partial(_add_t_kernel, normalize=normalize),
        grid=(s // bm,),
        in_specs=[pl.BlockSpec((bm, d), lambda i: (i, 0)),
                  pl.BlockSpec((d, bm), lambda i: (0, i)),
                  pl.BlockSpec((1, d), lambda i: (0, 0))],
        out_specs=pl.BlockSpec((bm, d), lambda i: (i, 0)),
        out_shape=jax.ShapeDtypeStruct((s, d), F32),
        compiler_params=_params(("parallel",), nbytes, 2 * _nbytes((bm, d), F32)),
        name="add_transposed",
    )(x, y_t, g.reshape(1, d))


def _layer(x, mem, p, *, last, norm_final):
    d = x.shape[1]
    sb_w = p["w_branch_sb"].shape[0]
    pool_w = p["w_branch_pool"].shape[0]
    n_sb_heads = sb_w // HEAD_DIM
    xa_w = p["xa_w_q"].shape[1]
    bf = lambda w: w.astype(BF16)

    w_in = bf(p["w_in"])
    h1 = _rmsnorm(x, p["norm_mix"], out_dtype=BF16)
    qkv = _matmul(h1, w_in, out_dtype=BF16, n_cols=3 * sb_w)
    u_pool = _matmul(h1, w_in, out_dtype=F32, col_start=3 * sb_w, n_cols=pool_w)
    gates = _matmul(h1, w_in, out_dtype=F32, epilogue="sigmoid", col_start=3 * sb_w + pool_w)
    attn = _sb_attention(qkv, n_heads=n_sb_heads)
    pooled = _multiscale_pool(u_pool, bf(p["pool_group_w"]), p["pool_scale"])
    mix = _gated_mix(attn, pooled, bf(p["w_branch_sb"]), bf(p["w_branch_pool"]), gates, d_model=d)
    x1 = _matmul(mix, bf(p["w_out"]), out_dtype=F32, epilogue="residual", residual=x)

    mem_n = _rmsnorm(mem, p["norm_mem_kv"], out_dtype=BF16)
    xkv = _matmul(mem_n, bf(p["xa_w_kv"]), out_dtype=BF16)
    x2, h3_t = _memory_block(x1, xkv, p["norm_mem_q"], bf(p["xa_w_q"]), bf(p["xa_w_o"]),
                             p["norm_ffn"], n_heads=xa_w // HEAD_DIM)

    q_t = _matmul(bf(p["peer_w_query"].T), h3_t, out_dtype=BF16)
    sel = _peer_select(q_t, bf(p["peer_sub_keys"]))
    y_t = _peer_dense(h3_t, bf(p["peer_down"]), p["peer_up"], sel)
    return _add_transposed(x2, y_t, norm_final, normalize=last)


_LAYER_PARAMS = ("norm_mix", "norm_mem_q", "norm_mem_kv", "norm_ffn", "w_in", "pool_group_w",
                 "pool_scale", "w_branch_sb", "w_branch_pool", "w_out", "xa_w_q", "xa_w_kv",
                 "xa_w_o", "peer_w_query", "peer_sub_keys", "peer_down", "peer_up")


def kernel(x, mem, norm_mix, norm_mem_q, norm_mem_kv, norm_ffn, norm_final, w_in, pool_group_w,
           pool_scale, w_branch_sb, w_branch_pool, w_out, xa_w_q, xa_w_kv, xa_w_o,
           peer_w_query, peer_sub_keys, peer_down, peer_up):
    stacked = dict(norm_mix=norm_mix, norm_mem_q=norm_mem_q, norm_mem_kv=norm_mem_kv,
                   norm_ffn=norm_ffn, w_in=w_in, pool_group_w=pool_group_w, pool_scale=pool_scale,
                   w_branch_sb=w_branch_sb, w_branch_pool=w_branch_pool, w_out=w_out,
                   xa_w_q=xa_w_q, xa_w_kv=xa_w_kv, xa_w_o=xa_w_o, peer_w_query=peer_w_query,
                   peer_sub_keys=peer_sub_keys, peer_down=peer_down, peer_up=peer_up)
    depth = w_in.shape[0]
    outs = []
    for b in range(x.shape[0]):
        xb = x[b]
        for l in range(depth):
            p = {name: stacked[name][l] for name in _LAYER_PARAMS}
            xb = _layer(xb, mem[b], p, last=(l == depth - 1), norm_final=norm_final)
        outs.append(xb)
    return jnp.stack(outs, axis=0)
```

```python
import functools
import math

import jax
import jax.numpy as jnp
from jax import lax
from jax.experimental import pallas as pl
from jax.experimental.pallas import tpu as pltpu

F32 = jnp.float32
BF16 = jnp.bfloat16

HEAD_DIM = 128
POOL_WINDOWS = (2, 4, 8, 16)
PEER_TOPK = 16
EPS = 1e-6

V7X_VMEM_BUDGET = 56 * 2**20
LANES = 128
SUBLANES = 8

SB_DECAY_STOP = 120.0
RANK_NONE = 1.0e9


def _vmem_limit(pipelined_bytes, scratch_bytes=0):
    need = 2 * pipelined_bytes + scratch_bytes + 12 * 2**20
    return int(min(V7X_VMEM_BUDGET, max(need, 16 * 2**20)))


def _nbytes(shape, dtype):
    return math.prod(shape) * jnp.dtype(dtype).itemsize


def _params(semantics, pipelined_bytes, scratch_bytes=0):
    return pltpu.CompilerParams(
        dimension_semantics=semantics,
        vmem_limit_bytes=_vmem_limit(pipelined_bytes, scratch_bytes))


def _tile(n, pref):
    t = min(n, pref)
    assert n % t == 0, (n, pref)
    return t


def _rmsnorm_rows(x, g):
    return x * lax.rsqrt(jnp.mean(x * x, axis=-1, keepdims=True) + EPS) * g


def _rmsnorm_kernel(x_ref, g_ref, o_ref):
    o_ref[...] = _rmsnorm_rows(x_ref[...], g_ref[...]).astype(o_ref.dtype)


def _rmsnorm(x, g, *, out_dtype):
    m, d = x.shape
    bm = _tile(m, 256)
    row = pl.BlockSpec((bm, d), lambda i: (i, 0))
    nbytes = _nbytes((bm, d), F32) + _nbytes((bm, d), out_dtype)
    return pl.pallas_call(
        _rmsnorm_kernel,
        grid=(m // bm,),
        in_specs=[row, pl.BlockSpec((1, d), lambda i: (0, 0))],
        out_specs=row,
        out_shape=jax.ShapeDtypeStruct((m, d), out_dtype),
        compiler_params=_params(("parallel",), nbytes, 2 * _nbytes((bm, d), F32)),
        name="rmsnorm",
    )(x, g.reshape(1, d))


def _mm_kernel(a_ref, w_ref, *rest, epilogue):
    o_ref = rest[-1]
    acc = jnp.dot(a_ref[...], w_ref[...], preferred_element_type=F32)
    if epilogue == "sigmoid":
        acc = jax.nn.sigmoid(acc)
    elif epilogue == "residual":
        acc = rest[0][...] + acc
    o_ref[...] = acc.astype(o_ref.dtype)


def _matmul(a, w, *, out_dtype, epilogue="none", residual=None, bm=1024, bn=1024,
            col_start=0, n_cols=None):
    m, k = a.shape
    k2, n_all = w.shape
    assert k == k2
    n = n_all - col_start if n_cols is None else n_cols
    bm = _tile(m, bm)
    bn = next(t for t in (bn, bn // 2, bn // 4, LANES) if n % t == 0 and col_start % t == 0)
    j0 = col_start // bn
    in_specs = [pl.BlockSpec((bm, k), lambda i, j: (i, 0)),
                pl.BlockSpec((k, bn), lambda i, j: (0, j0 + j))]
    args = [a, w]
    nbytes = _nbytes((bm, k), a.dtype) + _nbytes((k, bn), w.dtype) + _nbytes((bm, bn), out_dtype)
    if epilogue == "residual":
        in_specs.append(pl.BlockSpec((bm, bn), lambda i, j: (i, j)))
        args.append(residual)
        nbytes += _nbytes((bm, bn), residual.dtype)
    return pl.pallas_call(
        functools.partial(_mm_kernel, epilogue=epilogue),
        grid=(m // bm, n // bn),
        in_specs=in_specs,
        out_specs=pl.BlockSpec((bm, bn), lambda i, j: (i, j)),
        out_shape=jax.ShapeDtypeStruct((m, n), out_dtype),
        compiler_params=_params(("parallel", "parallel"), nbytes, _nbytes((bm, bn), F32)),
        name="matmul_" + epilogue,
    )(*args)


def _mix_kernel(a1_ref, a2_ref, w1_ref, w2_ref, g1_ref, g2_ref, o_ref):
    y1 = jnp.dot(a1_ref[...], w1_ref[...], preferred_element_type=F32)
    y2 = jnp.dot(a2_ref[...], w2_ref[...], preferred_element_type=F32)
    o_ref[...] = (g1_ref[...] * y1 + g2_ref[...] * y2).astype(o_ref.dtype)


def _gated_mix(attn, pooled, w_sb, w_pool, gates, *, d_model):
    s, k1 = attn.shape
    k2 = pooled.shape[1]
    bm = _tile(s, 1024)
    bn = _tile(d_model, 512)
    nj = d_model // bn
    nbytes = (_nbytes((bm, k1), BF16) + _nbytes((bm, k2), BF16) + _nbytes((k1, bn), BF16)
              + _nbytes((k2, bn), BF16) + 2 * _nbytes((bm, bn), F32) + _nbytes((bm, bn), BF16))
    return pl.pallas_call(
        _mix_kernel,
        grid=(s // bm, nj),
        in_specs=[pl.BlockSpec((bm, k1), lambda i, j: (i, 0)),
                  pl.BlockSpec((bm, k2), lambda i, j: (i, 0)),
                  pl.BlockSpec((k1, bn), lambda i, j: (0, j)),
                  pl.BlockSpec((k2, bn), lambda i, j: (0, j)),
                  pl.BlockSpec((bm, bn), lambda i, j: (i, j)),
                  pl.BlockSpec((bm, bn), lambda i, j, nj=nj: (i, j + nj))],
        out_specs=pl.BlockSpec((bm, bn), lambda i, j: (i, j)),
        out_shape=jax.ShapeDtypeStruct((s, d_model), BF16),
        compiler_params=_params(("parallel", "parallel"), nbytes, 3 * _nbytes((bm, bn), F32)),
        name="gated_mix",
    )(attn, pooled, w_sb, w_pool, gates, gates)


def _softplus(z):
    return jnp.maximum(z, 0.0) + jnp.log(1.0 + jnp.exp(-jnp.abs(z)))


def _sb_attn_kernel(q_ref, k_ref, v_ref, o_ref, acc_ref, carry_ref, *, tq, hp, scale):
    qi = pl.program_id(1)
    acc_ref[...] = jnp.zeros_like(acc_ref)
    carry_ref[...] = jnp.zeros_like(carry_ref)
    causal = (lax.broadcasted_iota(jnp.int32, (tq, tq), 1)
              < lax.broadcasted_iota(jnp.int32, (tq, tq), 0))
    suffix_mat = jnp.where(lax.broadcasted_iota(jnp.int32, (tq, tq), 0)
                           > lax.broadcasted_iota(jnp.int32, (tq, tq), 1), 1.0, 0.0).astype(BF16)

    def block(kb, diagonal):
        ks = pl.multiple_of(kb * tq, tq)
        heads = range(hp)
        cols = [pl.ds(h * HEAD_DIM, HEAD_DIM) for h in heads]
        z = [lax.dot_general(q_ref[:, cols[h]], k_ref[pl.ds(ks, tq), cols[h]],
                             (((1,), (1,)), ((), ())), preferred_element_type=F32) * scale
             for h in heads]
        spz = [_softplus(z[h]) for h in heads]
        sp = [jnp.where(causal, spz[h], 0.0) if diagonal else spz[h] for h in heads]
        sp_hi = [sp[h].astype(BF16) for h in heads]
        sp_lo = [(sp[h] - sp_hi[h].astype(F32)).astype(BF16) for h in heads]
        sums = [jnp.dot(sp_hi[h], suffix_mat, preferred_element_type=F32)
                + jnp.dot(sp_lo[h], suffix_mat, preferred_element_type=F32) for h in heads]
        smallest = None
        for h in heads:
            carry = carry_ref[h]
            decay = sums[h] + jnp.concatenate([carry] * (tq // LANES), axis=1)
            a = jnp.exp(z[h] - spz[h] - decay)
            if diagonal:
                a = jnp.where(causal, a, 0.0)
            acc_ref[:, cols[h]] += jnp.dot(a.astype(BF16), v_ref[pl.ds(ks, tq), cols[h]],
                                           preferred_element_type=F32)
            new_carry = carry + jnp.broadcast_to(sums[h][:, 0:1] + sp[h][:, 0:1], carry.shape)
            carry_ref[h] = new_carry
            least = jnp.min(new_carry)
            smallest = least if smallest is None else jnp.minimum(smallest, least)
        return smallest

    def cond(state):
        kb, min_carry = state
        return jnp.logical_and(kb >= 0, min_carry < SB_DECAY_STOP)

    def body(state):
        kb, _ = state
        return kb - 1, block(kb, False)

    lax.while_loop(cond, body, (qi - 1, block(qi, True)))
    o_ref[...] = acc_ref[...].astype(o_ref.dtype)


def _sb_attention(qkv, *, n_heads):
    s = qkv.shape[0]
    tq = _tile(s, 256)
    hp = next(c for c in (8, 4, 2, 1) if n_heads % c == 0)
    ng = n_heads // hp
    gw = hp * HEAD_DIM
    once = pl.Buffered(1)
    nbytes = 2 * _nbytes((tq, gw), BF16)
    scratch = (_nbytes((tq, gw), F32) + _nbytes((hp, tq, LANES), F32)
               + 2 * _nbytes((s, gw), BF16))
    return pl.pallas_call(
        functools.partial(_sb_attn_kernel, tq=tq, hp=hp, scale=HEAD_DIM ** -0.5),
        grid=(ng, s // tq),
        in_specs=[pl.BlockSpec((tq, gw), lambda g, i: (i, g)),
                  pl.BlockSpec((s, gw), lambda g, i, ng=ng: (0, ng + g), pipeline_mode=once),
                  pl.BlockSpec((s, gw), lambda g, i, ng=ng: (0, 2 * ng + g), pipeline_mode=once)],
        out_specs=pl.BlockSpec((tq, gw), lambda g, i: (i, g)),
        out_shape=jax.ShapeDtypeStruct((s, n_heads * HEAD_DIM), BF16),
        scratch_shapes=[pltpu.VMEM((tq, gw), F32), pltpu.VMEM((hp, tq, LANES), F32)],
        compiler_params=_params(("parallel", "arbitrary"), nbytes,
                                scratch + 12 * hp * _nbytes((tq, 2 * tq), F32)),
        name="sb_attention",
    )(qkv, qkv, qkv)


def _pool_kernel(u_ref, halo_ref, w_ref, scale_ref, o_ref, ext_ref, *, bm, gd, halo):
    i = pl.program_id(0)
    ext_ref[pl.ds(0, halo), :] = jnp.where(i > 0, halo_ref[...], 0.0)
    ext_ref[pl.ds(halo, bm), :] = u_ref[...]
    pos = i * bm + lax.broadcasted_iota(jnp.int32, (bm, gd), 0)
    for g, win in enumerate(POOL_WINDOWS):
        cols = pl.ds(g * gd, gd)
        tok = ext_ref[pl.ds(halo, bm), cols]
        total = tok
        for back in range(1, win):
            total = total + ext_ref[pl.ds(halo - back, bm), cols]
        count = jnp.minimum(pos + 1, win).astype(F32)
        pooled = (total / count - tok).astype(BF16)
        mixed = jnp.dot(pooled, w_ref[g], preferred_element_type=F32)
        o_ref[:, cols] = (mixed * scale_ref[:, cols]).astype(o_ref.dtype)


def _multiscale_pool(u, w_groups, pool_scale):
    s, pw = u.shape
    n_groups, gd, _ = w_groups.shape
    assert n_groups == len(POOL_WINDOWS)
    halo = 16
    assert halo >= max(POOL_WINDOWS) and s % halo == 0
    bm = _tile(s, 512)
    hb = bm // halo
    nbytes = (_nbytes((bm, pw), F32) + _nbytes((halo, pw), F32) + _nbytes(w_groups.shape, BF16)
              + _nbytes((bm, pw), BF16))
    scratch = _nbytes((bm + halo, pw), F32)
    return pl.pallas_call(
        functools.partial(_pool_kernel, bm=bm, gd=gd, halo=halo),
        grid=(s // bm,),
        in_specs=[pl.BlockSpec((bm, pw), lambda i: (i, 0)),
                  pl.BlockSpec((halo, pw), lambda i, hb=hb: (jnp.maximum(i * hb - 1, 0), 0)),
                  pl.BlockSpec((n_groups, gd, gd), lambda i: (0, 0, 0)),
                  pl.BlockSpec((1, pw), lambda i: (0, 0))],
        out_specs=pl.BlockSpec((bm, pw), lambda i: (i, 0)),
        out_shape=jax.ShapeDtypeStruct((s, pw), BF16),
        scratch_shapes=[pltpu.VMEM((bm + halo, pw), F32)],
        compiler_params=_params(("parallel",), nbytes, scratch + 4 * _nbytes((bm, gd), F32)),
        name="multiscale_pool",
    )(u, u, w_groups, pool_scale.reshape(1, pw))


def _mem_block_kernel(x_ref, gq_ref, wq_ref, kv_ref, wo_ref, gf_ref, x2_ref, ht_ref, att_ref,
                      *, n_heads, scale):
    width = n_heads * HEAD_DIM
    x1 = x_ref[...]
    hq = _rmsnorm_rows(x1, gq_ref[...]).astype(BF16)
    q = jnp.dot(hq, wq_ref[...], preferred_element_type=F32).astype(BF16)
    heads = range(n_heads)
    s = [lax.dot_general(q[:, h * HEAD_DIM:(h + 1) * HEAD_DIM], kv_ref[:, pl.ds(h * HEAD_DIM, HEAD_DIM)],
                         (((1,), (1,)), ((), ())), preferred_element_type=F32) * scale for h in heads]
    e = [jnp.exp(s[h] - jnp.max(s[h], axis=-1, keepdims=True)) for h in heads]
    p = [(e[h] / jnp.sum(e[h], axis=-1, keepdims=True)).astype(BF16) for h in heads]
    for h in heads:
        v = kv_ref[:, pl.ds(width + h * HEAD_DIM, HEAD_DIM)]
        att_ref[:, pl.ds(h * HEAD_DIM, HEAD_DIM)] = jnp.dot(
            p[h], v, preferred_element_type=F32).astype(att_ref.dtype)
    x2 = x1 + jnp.dot(att_ref[...], wo_ref[...], preferred_element_type=F32)
    x2_ref[...] = x2
    ht_ref[...] = _rmsnorm_rows(x2, gf_ref[...]).T.astype(ht_ref.dtype)


def _memory_block(x, kv, g_q, w_q, w_o, g_next, *, n_heads):
    s, d = x.shape
    m = kv.shape[0]
    width = n_heads * HEAD_DIM
    bm = _tile(s, 256)
    whole = lambda i: (0, 0)
    nbytes = (2 * _nbytes((bm, d), F32) + _nbytes((d, bm), BF16) + 2 * _nbytes((d, width), BF16)
              + _nbytes(kv.shape, BF16))
    scratch = _nbytes((bm, width), BF16) + 3 * _nbytes((bm, d), F32)
    return pl.pallas_call(
        functools.partial(_mem_block_kernel, n_heads=n_heads, scale=HEAD_DIM ** -0.5),
        grid=(s // bm,),
        in_specs=[pl.BlockSpec((bm, d), lambda i: (i, 0)),
                  pl.BlockSpec((1, d), whole),
                  pl.BlockSpec((d, width), whole),
                  pl.BlockSpec((m, 2 * width), whole),
                  pl.BlockSpec((width, d), whole),
                  pl.BlockSpec((1, d), whole)],
        out_specs=[pl.BlockSpec((bm, d), lambda i: (i, 0)),
                   pl.BlockSpec((d, bm), lambda i: (0, i))],
        out_shape=[jax.ShapeDtypeStruct((s, d), F32), jax.ShapeDtypeStruct((d, s), BF16)],
        scratch_shapes=[pltpu.VMEM((bm, width), BF16)],
        compiler_params=_params(("parallel",), nbytes, scratch),
        name="memory_block",
    )(x, g_q.reshape(1, d), w_q, kv, w_o, g_next.reshape(1, d))


def _topk_rows(score_list, k):
    row_ids = {s.shape: lax.broadcasted_iota(jnp.int32, s.shape, 0).astype(F32) for s in score_list}

    def body(r, carry):
        out = []
        for work, vals, idxs in carry:
            n, w = work.shape
            krows = lax.broadcasted_iota(jnp.int32, (k, w), 0)
            rid = row_ids[work.shape]
            parts = [(work[g:g + SUBLANES, :], rid[g:g + SUBLANES, :]) for g in range(0, n, SUBLANES)]
            while len(parts) > 1:
                merged = []
                for left in range(0, len(parts) - 1, 2):
                    (va, ia), (vb, ib) = parts[left], parts[left + 1]
                    merged.append((jnp.maximum(va, vb), jnp.where(va >= vb, ia, ib)))
                if len(parts) % 2:
                    merged.append(parts[-1])
                parts = merged
            v8, i8 = parts[0]
            top = jnp.max(v8, axis=0, keepdims=True)
            idx = jnp.min(jnp.where(v8 == top, i8, float(n)), axis=0, keepdims=True)
            work = jnp.where(rid == idx, -jnp.inf, work)
            vals = jnp.where(krows == r, top, vals)
            idxs = jnp.where(krows == r, idx, idxs)
            out.append((work, vals, idxs))
        return tuple(out)

    init = tuple((s, jnp.zeros((k, s.shape[1]), F32), jnp.zeros((k, s.shape[1]), F32))
                 for s in score_list)
    return [(vals, idxs) for _, vals, idxs in lax.fori_loop(0, k, body, init)]


def _peer_select_kernel(q_ref, keys_ref, a_ref, nb_ref, b_ref, rb_ref, *, half, tn):
    k = PEER_TOPK
    hk = k // 2
    n_keys = a_ref.shape[0]

    def rank_keys(lanes):
        s1 = jnp.dot(keys_ref[0], q_ref[pl.ds(0, half), lanes], preferred_element_type=F32)
        s2 = jnp.dot(keys_ref[1], q_ref[pl.ds(half, half), lanes], preferred_element_type=F32)
        (v1, i1), (v2, i2) = _topk_rows([s1, s2], k)
        cand = jnp.concatenate(
            [v1[0:1, :] + v2] + [v1[a:a + 1, :] + v2[0:hk, :] for a in range(1, hk)]
            + [v1[hk:k, :] + v2[0:1, :]], axis=0)
        return s1, s2, v1, i1, v2, i2, cand

    groups = [pl.ds(cg * LANES, LANES) for cg in range(tn // LANES)]
    ranked = {}
    for cg, lanes in enumerate(groups):
        if cg % 2 == 0:
            pair = groups[cg:cg + 2]
            keyed = [rank_keys(g) for g in pair]
            tops = _topk_rows([kd[-1] for kd in keyed], k)
            for off, (kd, top) in enumerate(zip(keyed, tops)):
                ranked[cg + off] = kd + top
        s1, s2, v1, i1, v2, i2, cand, best, bpos = ranked.pop(cg)
        tau = best[k - 1:k, :]
        last = bpos[k - 1:k, :]
        pos = lax.broadcasted_iota(jnp.int32, cand.shape, 0).astype(F32)
        picked = jnp.where(cand > tau, 1.0,
                           jnp.where(cand == tau, jnp.where(pos <= last, 1.0, 0.0), 0.0))
        nb = [jnp.sum(picked[0:k, :], axis=0, keepdims=True)]
        nb += [jnp.sum(picked[k + (a - 1) * hk:k + a * hk, :], axis=0, keepdims=True)
               for a in range(1, hk)]
        tail = k + (hk - 1) * hk
        nb += [picked[tail + a:tail + a + 1, :] for a in range(k - hk)]
        rows = lax.broadcasted_iota(jnp.int32, (n_keys, LANES), 0).astype(F32)
        nb_dense = jnp.zeros((n_keys, LANES), F32)
        rb = jnp.full((n_keys, LANES), RANK_NONE, F32)
        for r in range(k):
            nb_dense = jnp.where(rows == i1[r:r + 1, :], nb[r], nb_dense)
            rb = jnp.where(rows == i2[r:r + 1, :], float(r), rb)
        z = jnp.sum(jnp.exp(best - best[0:1, :]), axis=0, keepdims=True)
        a_ref[:, lanes] = jnp.exp(s1 - v1[0:1, :])
        b_ref[:, lanes] = jnp.exp(s2 - v2[0:1, :]) / z
        nb_ref[:, lanes] = nb_dense
        rb_ref[:, lanes] = rb


def _peer_select(q_t, sub_keys):
    n_heads, _, n_keys, half = sub_keys.shape
    s = q_t.shape[1]
    assert PEER_TOPK % (2 * SUBLANES) == 0 and n_keys >= PEER_TOPK
    tn = _tile(s, 512)
    dense = jax.ShapeDtypeStruct((n_heads, n_keys, s), F32)
    dense_spec = pl.BlockSpec((None, n_keys, tn), lambda i, h: (h, 0, i))
    nbytes = (_nbytes((2 * half, tn), BF16) + _nbytes((2, n_keys, half), BF16)
              + 4 * _nbytes((n_keys, tn), F32))
    return pl.pallas_call(
        functools.partial(_peer_select_kernel, half=half, tn=tn),
        grid=(s // tn, n_heads),
        in_specs=[pl.BlockSpec((2 * half, tn), lambda i, h: (h, i)),
                  pl.BlockSpec((None, 2, n_keys, half), lambda i, h: (h, 0, 0, 0))],
        out_specs=[dense_spec] * 4,
        out_shape=[dense] * 4,
        compiler_params=_params(("parallel", "parallel"), nbytes, 8 * 2**20),
        name="peer_select",
    )(q_t, sub_keys)


def _gelu(x):
    return 0.5 * x * (1.0 + lax.erf(x * (2.0 ** -0.5)))


def _peer_dense_kernel(ht_ref, dn_ref, upt_ref, a_ref, nb_ref, b_ref, rb_ref, o_ref,
                       pt0_ref, pt1_ref, wt0_ref, wt1_ref,
                       *, n_heads, n_keys, n_tiles, te, tn, irows, jc, pieces):
    t = pl.program_id(0)
    n_pairs = pl.num_programs(0) - 2
    rps = te // n_keys
    gate_tile = jnp.clip(t - 1, 0, n_pairs - 1) % n_tiles
    ioff = (gate_tile * rps) % irows
    chunks_per_i = n_keys // jc

    @pl.when(t == 0)
    def _():
        pt1_ref[...] = jnp.zeros_like(pt1_ref)
        wt0_ref[...] = jnp.zeros_like(wt0_ref)

    @pl.when(jnp.logical_or(t == 0, (t - 2) % n_tiles == 0))
    def _():
        o_ref[...] = jnp.zeros_like(o_ref)

    def step(pt_new, pt_old, wt_new, wt_old):
        def gate_chunk(c):
            irow = ioff + c // chunks_per_i
            jrows = pl.ds((c % chunks_per_i) * jc, jc)
            gate = jnp.zeros((jc, tn), F32)
            for h in range(n_heads):
                prod = a_ref[h, pl.ds(irow, 1), :] * b_ref[h, jrows, :]
                picked = rb_ref[h, jrows, :] < nb_ref[h, pl.ds(irow, 1), :]
                gate = gate + jnp.where(picked, prod, 0.0)
            act = _gelu(pt_old[pl.ds(c * jc, jc), :])
            wt_new[pl.ds(c * jc, jc), :] = (gate * act).astype(wt_new.dtype)

        n_chunks = te // jc
        d_per = ht_ref.shape[0] // pieces
        for p in range(pieces):
            span = pl.ds(p * d_per, d_per)
            o_ref[span, :] += jnp.dot(upt_ref[span, :], wt_old[...], preferred_element_type=F32)
            for c in range(2 * p * n_chunks // (2 * pieces), (2 * p + 1) * n_chunks // (2 * pieces)):
                gate_chunk(c)
            part = jnp.dot(dn_ref[:, span], ht_ref[span, :], preferred_element_type=F32)
            if p == 0:
                pt_new[...] = part
            else:
                pt_new[...] += part
            for c in range((2 * p + 1) * n_chunks // (2 * pieces), (2 * p + 2) * n_chunks // (2 * pieces)):
                gate_chunk(c)

    @pl.when(t % 2 == 0)
    def _():
        step(pt0_ref, pt1_ref, wt1_ref, wt0_ref)

    @pl.when(t % 2 == 1)
    def _():
        step(pt1_ref, pt0_ref, wt0_ref, wt1_ref)


def _peer_dense(h_t, down, up, sel):
    a, nb, b, rb = sel
    n_heads, n_keys, s = a.shape
    d = h_t.shape[0]
    n_exp = down.shape[0]
    tn = _tile(s, 512)
    te = _tile(n_exp, 512)
    assert te % n_keys == 0
    n_tiles = n_exp // te
    rps = te // n_keys
    irows = max(rps, SUBLANES)
    assert irows % rps == 0 and n_keys % irows == 0
    jc = 16
    pieces = next(p for p in (16, 8, 4, 2, 1)
                  if d % (p * LANES) == 0 and (te // jc) % (2 * p) == 0)
    up_t = jnp.transpose(up.reshape(n_tiles, te, d), (0, 2, 1)).astype(BF16)
    n_pairs = (s // tn) * n_tiles
    pair = lambda t, lag: jnp.clip(t - lag, 0, n_pairs - 1)
    tok = lambda t, lag: pair(t, lag) // n_tiles
    exp = lambda t, lag: pair(t, lag) % n_tiles
    once = pl.Buffered(1)
    by_i = pl.BlockSpec((n_heads, irows, tn), lambda t: (0, (exp(t, 1) * rps) // irows, tok(t, 1)))
    by_j = pl.BlockSpec((n_heads, n_keys, tn), lambda t: (0, 0, tok(t, 1)), pipeline_mode=once)
    nbytes = (2 * _nbytes((te, d), BF16) + 2 * _nbytes((n_heads, irows, tn), F32)
              + _nbytes((d, tn), F32) + _nbytes((d, tn), BF16))
    scratch = (2 * _nbytes((te, tn), F32) + 2 * _nbytes((te, tn), BF16)
               + 2 * _nbytes((n_heads, n_keys, tn), F32))
    return pl.pallas_call(
        functools.partial(_peer_dense_kernel, n_heads=n_heads, n_keys=n_keys, n_tiles=n_tiles,
                          te=te, tn=tn, irows=irows, jc=jc, pieces=pieces),
        grid=(n_pairs + 2,),
        in_specs=[pl.BlockSpec((d, tn), lambda t: (0, tok(t, 0))),
                  pl.BlockSpec((te, d), lambda t: (exp(t, 0), 0)),
                  pl.BlockSpec((None, d, te), lambda t: (exp(t, 2), 0, 0)),
                  by_i, by_i, by_j, by_j],
        out_specs=pl.BlockSpec((d, tn), lambda t: (0, tok(t, 2))),
        out_shape=jax.ShapeDtypeStruct((d, s), F32),
        scratch_shapes=[pltpu.VMEM((te, tn), F32), pltpu.VMEM((te, tn), F32),
                        pltpu.VMEM((te, tn), BF16), pltpu.VMEM((te, tn), BF16)],
        compiler_params=_params(("arbitrary",), nbytes, scratch),
        name="peer_dense",
    )(h_t, down, up_t, a, nb, b, rb)


def _add_t_kernel(x_ref, yt_ref, g_ref, o_ref, *, normalize):
    x = x_ref[...] + yt_ref[...].T
    if normalize:
        x = _rmsnorm_rows(x, g_ref[...])
    o_ref[...] = x


def _add_transposed(x, y_t, g, *, normalize):
    s, d = x.shape
    bm = _tile(s, 256)
    nbytes = 3 * _nbytes((bm, d), F32)
    return pl.pallas_call(
        functools.partial(_add_t_kernel, normalize=normalize),
        grid=(s // bm,),
        in_specs=[pl.BlockSpec((bm, d), lambda i: (i, 0)),
                  pl.BlockSpec((d, bm), lambda i: (0, i)),
                  pl.BlockSpec((1, d), lambda i: (0, 0))],
        out_specs=pl.BlockSpec((bm, d), lambda i: (i, 0)),
        out_shape=jax.ShapeDtypeStruct((s, d), F32),
        compiler_params=_params(("parallel",), nbytes, 2 * _nbytes((bm, d), F32)),
        name="add_transposed",
    )(x, y_t, g.reshape(1, d))


def _layer(x, mem, p, *, last, norm_final):
    d = x.shape[1]
    sb_w = p["w_branch_sb"].shape[0]
    pool_w = p["w_branch_pool"].shape[0]
    n_sb_heads = sb_w // HEAD_DIM
    xa_w = p["xa_w_q"].shape[1]
    bf = lambda w: w.astype(BF16)

    w_in = bf(p["w_in"])
    h1 = _rmsnorm(x, p["norm_mix"], out_dtype=BF16)
    qkv = _matmul(h1, w_in, out_dtype=BF16, n_cols=3 * sb_w)
    u_pool = _matmul(h1, w_in, out_dtype=F32, col_start=3 * sb_w, n_cols=pool_w)
    gates = _matmul(h1, w_in, out_dtype=F32, epilogue="sigmoid", col_start=3 * sb_w + pool_w)
    attn = _sb_attention(qkv, n_heads=n_sb_heads)
    pooled = _multiscale_pool(u_pool, bf(p["pool_group_w"]), p["pool_scale"])
    mix = _gated_mix(attn, pooled, bf(p["w_branch_sb"]), bf(p["w_branch_pool"]), gates, d_model=d)
    x1 = _matmul(mix, bf(p["w_out"]), out_dtype=F32, epilogue="residual", residual=x)

    mem_n = _rmsnorm(mem, p["norm_mem_kv"], out_dtype=BF16)
    xkv = _matmul(mem_n, bf(p["xa_w_kv"]), out_dtype=BF16)
    x2, h3_t = _memory_block(x1, xkv, p["norm_mem_q"], bf(p["xa_w_q"]), bf(p["xa_w_o"]),
                             p["norm_ffn"], n_heads=xa_w // HEAD_DIM)

    q_t = _matmul(bf(p["peer_w_query"].T), h3_t, out_dtype=BF16)
    sel = _peer_select(q_t, bf(p["peer_sub_keys"]))
    y_t = _peer_dense(h3_t, bf(p["peer_down"]), p["peer_up"], sel)
    return _add_transposed(x2, y_t, norm_final, normalize=last)


_LAYER_PARAMS = ("norm_mix", "norm_mem_q", "norm_mem_kv", "norm_ffn", "w_in", "pool_group_w",
                 "pool_scale", "w_branch_sb", "w_branch_pool", "w_out", "xa_w_q", "xa_w_kv",
                 "xa_w_o", "peer_w_query", "peer_sub_keys", "peer_down", "peer_up")


def kernel(x, mem, norm_mix, norm_mem_q, norm_mem_kv, norm_ffn, norm_final, w_in, pool_group_w,
           pool_scale, w_branch_sb, w_branch_pool, w_out, xa_w_q, xa_w_kv, xa_w_o,
           peer_w_query, peer_sub_keys, peer_down, peer_up):
    stacked = dict(norm_mix=norm_mix, norm_mem_q=norm_mem_q, norm_mem_kv=norm_mem_kv,
                   norm_ffn=norm_ffn, w_in=w_in, pool_group_w=pool_group_w, pool_scale=pool_scale,
                   w_branch_sb=w_branch_sb, w_branch_pool=w_branch_pool, w_out=w_out,
                   xa_w_q=xa_w_q, xa_w_kv=xa_w_kv, xa_w_o=xa_w_o, peer_w_query=peer_w_query,
                   peer_sub_keys=peer_sub_keys, peer_down=peer_down, peer_up=peer_up)
    depth = w_in.shape[0]
    outs = []
    for b in range(x.shape[0]):
        xb = x[b]
        for l in range(depth):
            p = {name: stacked[name][l] for name in _LAYER_PARAMS}
            xb = _layer(xb, mem[b], p, last=(l == depth - 1), norm_final=norm_final)
        outs.append(xb)
    return jnp.stack(outs, axis=0)
```

```python
import functools
import math

import jax
import jax.numpy as jnp
from jax import lax
from jax.experimental import pallas as pl
from jax.experimental.pallas import tpu as pltpu

F32 = jnp.float32
BF16 = jnp.bfloat16

HEAD_DIM = 128
POOL_WINDOWS = (2, 4, 8, 16)
PEER_TOPK = 16
EPS = 1e-6

V7X_VMEM_BUDGET = 56 * 2**20
LANES = 128
SUBLANES = 8

SB_DECAY_STOP = 120.0
RANK_NONE = 1.0e9


def _vmem_limit(pipelined_bytes, scratch_bytes=0):
    need = 2 * pipelined_bytes + scratch_bytes + 12 * 2**20
    return int(min(V7X_VMEM_BUDGET, max(need, 16 * 2**20)))


def _nbytes(shape, dtype):
    return math.prod(shape) * jnp.dtype(dtype).itemsize


def _params(semantics, pipelined_bytes, scratch_bytes=0):
    return pltpu.CompilerParams(
        dimension_semantics=semantics,
        vmem_limit_bytes=_vmem_limit(pipelined_bytes, scratch_bytes))


def _tile(n, pref):
    t = min(n, pref)
    assert n % t == 0, (n, pref)
    return t


def _rmsnorm_rows(x, g):
    return x * lax.rsqrt(jnp.mean(x * x, axis=-1, keepdims=True) + EPS) * g


def _rmsnorm_kernel(x_ref, g_ref, o_ref):
    o_ref[...] = _rmsnorm_rows(x_ref[...], g_ref[...]).astype(o_ref.dtype)


def _rmsnorm(x, g, *, out_dtype):
    m, d = x.shape
    bm = _tile(m, 256)
    row = pl.BlockSpec((bm, d), lambda i: (i, 0))
    nbytes = _nbytes((bm, d), F32) + _nbytes((bm, d), out_dtype)
    return pl.pallas_call(
        _rmsnorm_kernel,
        grid=(m // bm,),
        in_specs=[row, pl.BlockSpec((1, d), lambda i: (0, 0))],
        out_specs=row,
        out_shape=jax.ShapeDtypeStruct((m, d), out_dtype),
        compiler_params=_params(("parallel",), nbytes, 2 * _nbytes((bm, d), F32)),
        name="rmsnorm",
    )(x, g.reshape(1, d))


def _mm_kernel(a_ref, w_ref, *rest, epilogue):
    o_ref = rest[-1]
    acc = jnp.dot(a_ref[...], w_ref[...], preferred_element_type=F32)
    if epilogue == "sigmoid":
        acc = jax.nn.sigmoid(acc)
    elif epilogue == "residual":
        acc = rest[0][...] + acc
    o_ref[...] = acc.astype(o_ref.dtype)


def _matmul(a, w, *, out_dtype, epilogue="none", residual=None, bm=1024, bn=1024,
            col_start=0, n_cols=None):
    m, k = a.shape
    k2, n_all = w.shape
    assert k == k2
    n = n_all - col_start if n_cols is None else n_cols
    bm = _tile(m, bm)
    bn = next(t for t in (bn, bn // 2, bn // 4, LANES) if n % t == 0 and col_start % t == 0)
    j0 = col_start // bn
    in_specs = [pl.BlockSpec((bm, k), lambda i, j: (i, 0)),
                pl.BlockSpec((k, bn), lambda i, j: (0, j0 + j))]
    args = [a, w]
    nbytes = _nbytes((bm, k), a.dtype) + _nbytes((k, bn), w.dtype) + _nbytes((bm, bn), out_dtype)
    if epilogue == "residual":
        in_specs.append(pl.BlockSpec((bm, bn), lambda i, j: (i, j)))
        args.append(residual)
        nbytes += _nbytes((bm, bn), residual.dtype)
    return pl.pallas_call(
        functools.partial(_mm_kernel, epilogue=epilogue),
        grid=(m // bm, n // bn),
        in_specs=in_specs,
        out_specs=pl.BlockSpec((bm, bn), lambda i, j: (i, j)),
        out_shape=jax.ShapeDtypeStruct((m, n), out_dtype),
        compiler_params=_params(("parallel", "parallel"), nbytes, _nbytes((bm, bn), F32)),
        name="matmul_" + epilogue,
    )(*args)


def _mix_kernel(a1_ref, a2_ref, w1_ref, w2_ref, g1_ref, g2_ref, o_ref):
    y1 = jnp.dot(a1_ref[...], w1_ref[...].astype(BF16), preferred_element_type=F32)
    y2 = jnp.dot(a2_ref[...], w2_ref[...].astype(BF16), preferred_element_type=F32)
    o_ref[...] = (g1_ref[...] * y1 + g2_ref[...] * y2).astype(o_ref.dtype)


def _gated_mix(attn, pooled, w_sb, w_pool, gates, *, d_model):
    s, k1 = attn.shape
    k2 = pooled.shape[1]
    bm = _tile(s, 1024)
    bn = _tile(d_model, 512)
    nj = d_model // bn
    nbytes = (_nbytes((bm, k1), BF16) + _nbytes((bm, k2), BF16) + _nbytes((k1, bn), w_sb.dtype)
              + _nbytes((k2, bn), w_pool.dtype) + 2 * _nbytes((bm, bn), F32) + _nbytes((bm, bn), BF16))
    return pl.pallas_call(
        _mix_kernel,
        grid=(s // bm, nj),
        in_specs=[pl.BlockSpec((bm, k1), lambda i, j: (i, 0)),
                  pl.BlockSpec((bm, k2), lambda i, j: (i, 0)),
                  pl.BlockSpec((k1, bn), lambda i, j: (0, j)),
                  pl.BlockSpec((k2, bn), lambda i, j: (0, j)),
                  pl.BlockSpec((bm, bn), lambda i, j: (i, j)),
                  pl.BlockSpec((bm, bn), lambda i, j, nj=nj: (i, j + nj))],
        out_specs=pl.BlockSpec((bm, bn), lambda i, j: (i, j)),
        out_shape=jax.ShapeDtypeStruct((s, d_model), BF16),
        compiler_params=_params(("parallel", "parallel"), nbytes, 3 * _nbytes((bm, bn), F32)),
        name="gated_mix",
    )(attn, pooled, w_sb, w_pool, gates, gates)


def _softplus(z):
    return jnp.maximum(z, 0.0) + jnp.log(1.0 + jnp.exp(-jnp.abs(z)))


def _sb_attn_kernel(q_ref, k_ref, v_ref, o_ref, acc_ref, carry_ref, *, tq, hp, scale):
    qi = pl.program_id(1)
    acc_ref[...] = jnp.zeros_like(acc_ref)
    carry_ref[...] = jnp.zeros_like(carry_ref)
    causal = (lax.broadcasted_iota(jnp.int32, (tq, tq), 1)
              < lax.broadcasted_iota(jnp.int32, (tq, tq), 0))
    suffix_mat = jnp.where(lax.broadcasted_iota(jnp.int32, (tq, tq), 0)
                           > lax.broadcasted_iota(jnp.int32, (tq, tq), 1), 1.0, 0.0).astype(BF16)

    def block(kb, diagonal):
        ks = pl.multiple_of(kb * tq, tq)
        heads = range(hp)
        cols = [pl.ds(h * HEAD_DIM, HEAD_DIM) for h in heads]
        z = [lax.dot_general(q_ref[:, cols[h]], k_ref[pl.ds(ks, tq), cols[h]],
                             (((1,), (1,)), ((), ())), preferred_element_type=F32) * scale
             for h in heads]
        spz = [_softplus(z[h]) for h in heads]
        sp = [jnp.where(causal, spz[h], 0.0) if diagonal else spz[h] for h in heads]
        sp_hi = [sp[h].astype(BF16) for h in heads]
        sp_lo = [(sp[h] - sp_hi[h].astype(F32)).astype(BF16) for h in heads]
        sums = [jnp.dot(sp_hi[h], suffix_mat, preferred_element_type=F32)
                + jnp.dot(sp_lo[h], suffix_mat, preferred_element_type=F32) for h in heads]
        smallest = None
        for h in heads:
            carry = carry_ref[h]
            decay = sums[h] + jnp.concatenate([carry] * (tq // LANES), axis=1)
            a = jnp.exp(z[h] - spz[h] - decay)
            if diagonal:
                a = jnp.where(causal, a, 0.0)
            acc_ref[:, cols[h]] += jnp.dot(a.astype(BF16), v_ref[pl.ds(ks, tq), cols[h]],
                                           preferred_element_type=F32)
            new_carry = carry + jnp.broadcast_to(sums[h][:, 0:1] + sp[h][:, 0:1], carry.shape)
            carry_ref[h] = new_carry
            least = jnp.min(new_carry)
            smallest = least if smallest is None else jnp.minimum(smallest, least)
        return smallest

    def cond(state):
        kb, min_carry = state
        return jnp.logical_and(kb >= 0, min_carry < SB_DECAY_STOP)

    def body(state):
        kb, _ = state
        return kb - 1, block(kb, False)

    lax.while_loop(cond, body, (qi - 1, block(qi, True)))
    o_ref[...] = acc_ref[...].astype(o_ref.dtype)


def _sb_attention(qkv, *, n_heads):
    s = qkv.shape[0]
    tq = _tile(s, 256)
    hp = next(c for c in (8, 4, 2, 1) if n_heads % c == 0)
    ng = n_heads // hp
    gw = hp * HEAD_DIM
    once = pl.Buffered(1)
    nbytes = 2 * _nbytes((tq, gw), BF16)
    scratch = (_nbytes((tq, gw), F32) + _nbytes((hp, tq, LANES), F32)
               + 2 * _nbytes((s, gw), BF16))
    return pl.pallas_call(
        functools.partial(_sb_attn_kernel, tq=tq, hp=hp, scale=HEAD_DIM ** -0.5),
        grid=(ng, s // tq),
        in_specs=[pl.BlockSpec((tq, gw), lambda g, i: (i, g)),
                  pl.BlockSpec((s, gw), lambda g, i, ng=ng: (0, ng + g), pipeline_mode=once),
                  pl.BlockSpec((s, gw), lambda g, i, ng=ng: (0, 2 * ng + g), pipeline_mode=once)],
        out_specs=pl.BlockSpec((tq, gw), lambda g, i: (i, g)),
        out_shape=jax.ShapeDtypeStruct((s, n_heads * HEAD_DIM), BF16),
        scratch_shapes=[pltpu.VMEM((tq, gw), F32), pltpu.VMEM((hp, tq, LANES), F32)],
        compiler_params=_params(("parallel", "arbitrary"), nbytes,
                                scratch + 12 * hp * _nbytes((tq, 2 * tq), F32)),
        name="sb_attention",
    )(qkv, qkv, qkv)


def _pool_kernel(u_ref, halo_ref, w_ref, scale_ref, o_ref, ext_ref, *, bm, gd, halo):
    i = pl.program_id(0)
    ext_ref[pl.ds(0, halo), :] = jnp.where(i > 0, halo_ref[...], 0.0)
    ext_ref[pl.ds(halo, bm), :] = u_ref[...]
    pos = i * bm + lax.broadcasted_iota(jnp.int32, (bm, gd), 0)
    for g, win in enumerate(POOL_WINDOWS):
        cols = pl.ds(g * gd, gd)
        tok = ext_ref[pl.ds(halo, bm), cols]
        total = tok
        for back in range(1, win):
            total = total + ext_ref[pl.ds(halo - back, bm), cols]
        count = jnp.minimum(pos + 1, win).astype(F32)
        pooled = (total / count - tok).astype(BF16)
        mixed = jnp.dot(pooled, w_ref[g], preferred_element_type=F32)
        o_ref[:, cols] = (mixed * scale_ref[:, cols]).astype(o_ref.dtype)


def _multiscale_pool(u, w_groups, pool_scale):
    s, pw = u.shape
    n_groups, gd, _ = w_groups.shape
    assert n_groups == len(POOL_WINDOWS)
    halo = 16
    assert halo >= max(POOL_WINDOWS) and s % halo == 0
    bm = _tile(s, 512)
    hb = bm // halo
    nbytes = (_nbytes((bm, pw), F32) + _nbytes((halo, pw), F32) + _nbytes(w_groups.shape, BF16)
              + _nbytes((bm, pw), BF16))
    scratch = _nbytes((bm + halo, pw), F32)
    return pl.pallas_call(
        functools.partial(_pool_kernel, bm=bm, gd=gd, halo=halo),
        grid=(s // bm,),
        in_specs=[pl.BlockSpec((bm, pw), lambda i: (i, 0)),
                  pl.BlockSpec((halo, pw), lambda i, hb=hb: (jnp.maximum(i * hb - 1, 0), 0)),
                  pl.BlockSpec((n_groups, gd, gd), lambda i: (0, 0, 0)),
                  pl.BlockSpec((1, pw), lambda i: (0, 0))],
        out_specs=pl.BlockSpec((bm, pw), lambda i: (i, 0)),
        out_shape=jax.ShapeDtypeStruct((s, pw), BF16),
        scratch_shapes=[pltpu.VMEM((bm + halo, pw), F32)],
        compiler_params=_params(("parallel",), nbytes, scratch + 4 * _nbytes((bm, gd), F32)),
        name="multiscale_pool",
    )(u, u, w_groups, pool_scale.reshape(1, pw))


def _mem_block_kernel(x_ref, gq_ref, wq_ref, kv_ref, wo_ref, gf_ref, x2_ref, ht_ref, att_ref,
                      *, n_heads, scale):
    width = n_heads * HEAD_DIM
    x1 = x_ref[...]
    hq = _rmsnorm_rows(x1, gq_ref[...]).astype(BF16)
    q = jnp.dot(hq, wq_ref[...], preferred_element_type=F32).astype(BF16)
    heads = range(n_heads)
    s = [lax.dot_general(q[:, h * HEAD_DIM:(h + 1) * HEAD_DIM], kv_ref[:, pl.ds(h * HEAD_DIM, HEAD_DIM)],
                         (((1,), (1,)), ((), ())), preferred_element_type=F32) * scale for h in heads]
    e = [jnp.exp(s[h] - jnp.max(s[h], axis=-1, keepdims=True)) for h in heads]
    p = [(e[h] / jnp.sum(e[h], axis=-1, keepdims=True)).astype(BF16) for h in heads]
    for h in heads:
        v = kv_ref[:, pl.ds(width + h * HEAD_DIM, HEAD_DIM)]
        att_ref[:, pl.ds(h * HEAD_DIM, HEAD_DIM)] = jnp.dot(
            p[h], v, preferred_element_type=F32).astype(att_ref.dtype)
    x2 = x1 + jnp.dot(att_ref[...], wo_ref[...], preferred_element_type=F32)
    x2_ref[...] = x2
    ht_ref[...] = _rmsnorm_rows(x2, gf_ref[...]).T.astype(ht_ref.dtype)


def _memory_block(x, kv, g_q, w_q, w_o, g_next, *, n_heads):
    s, d = x.shape
    m = kv.shape[0]
    width = n_heads * HEAD_DIM
    bm = _tile(s, 256)
    whole = lambda i: (0, 0)
    nbytes = (2 * _nbytes((bm, d), F32) + _nbytes((d, bm), BF16) + 2 * _nbytes((d, width), BF16)
              + _nbytes(kv.shape, BF16))
    scratch = _nbytes((bm, width), BF16) + 3 * _nbytes((bm, d), F32)
    return pl.pallas_call(
        functools.partial(_mem_block_kernel, n_heads=n_heads, scale=HEAD_DIM ** -0.5),
        grid=(s // bm,),
        in_specs=[pl.BlockSpec((bm, d), lambda i: (i, 0)),
                  pl.BlockSpec((1, d), whole),
                  pl.BlockSpec((d, width), whole),
                  pl.BlockSpec((m, 2 * width), whole),
                  pl.BlockSpec((width, d), whole),
                  pl.BlockSpec((1, d), whole)],
        out_specs=[pl.BlockSpec((bm, d), lambda i: (i, 0)),
                   pl.BlockSpec((d, bm), lambda i: (0, i))],
        out_shape=[jax.ShapeDtypeStruct((s, d), F32), jax.ShapeDtypeStruct((d, s), BF16)],
        scratch_shapes=[pltpu.VMEM((bm, width), BF16)],
        compiler_params=_params(("parallel",), nbytes, scratch),
        name="memory_block",
    )(x, g_q.reshape(1, d), w_q, kv, w_o, g_next.reshape(1, d))


def _topk_rows(score_list, k):
    row_ids = {s.shape: lax.broadcasted_iota(jnp.int32, s.shape, 0).astype(F32) for s in score_list}

    def body(r, carry):
        out = []
        for work, vals, idxs in carry:
            n, w = work.shape
            krows = lax.broadcasted_iota(jnp.int32, (k, w), 0)
            rid = row_ids[work.shape]
            parts = [(work[g:g + SUBLANES, :], rid[g:g + SUBLANES, :]) for g in range(0, n, SUBLANES)]
            while len(parts) > 1:
                merged = []
                for left in range(0, len(parts) - 1, 2):
                    (va, ia), (vb, ib) = parts[left], parts[left + 1]
                    merged.append((jnp.maximum(va, vb), jnp.where(va >= vb, ia, ib)))
                if len(parts) % 2:
                    merged.append(parts[-1])
                parts = merged
            v8, i8 = parts[0]
            top = jnp.max(v8, axis=0, keepdims=True)
            idx = jnp.min(jnp.where(v8 == top, i8, float(n)), axis=0, keepdims=True)
            work = jnp.where(rid == idx, -jnp.inf, work)
            vals = jnp.where(krows == r, top, vals)
            idxs = jnp.where(krows == r, idx, idxs)
            out.append((work, vals, idxs))
        return tuple(out)

    init = tuple((s, jnp.zeros((k, s.shape[1]), F32), jnp.zeros((k, s.shape[1]), F32))
                 for s in score_list)
    return [(vals, idxs) for _, vals, idxs in lax.fori_loop(0, k, body, init)]


def _peer_select_kernel(q_ref, keys_ref, a_ref, nb_ref, b_ref, rb_ref, *, half, tn):
    k = PEER_TOPK
    hk = k // 2
    n_keys = a_ref.shape[0]

    def rank_keys(lanes):
        s1 = jnp.dot(keys_ref[0], q_ref[pl.ds(0, half), lanes], preferred_element_type=F32)
        s2 = jnp.dot(keys_ref[1], q_ref[pl.ds(half, half), lanes], preferred_element_type=F32)
        (v1, i1), (v2, i2) = _topk_rows([s1, s2], k)
        cand = jnp.concatenate(
            [v1[0:1, :] + v2] + [v1[a:a + 1, :] + v2[0:hk, :] for a in range(1, hk)]
            + [v1[hk:k, :] + v2[0:1, :]], axis=0)
        return s1, s2, v1, i1, v2, i2, cand

    groups = [pl.ds(cg * LANES, LANES) for cg in range(tn // LANES)]
    ranked = {}
    for cg, lanes in enumerate(groups):
        if cg % 2 == 0:
            pair = groups[cg:cg + 2]
            keyed = [rank_keys(g) for g in pair]
            tops = _topk_rows([kd[-1] for kd in keyed], k)
            for off, (kd, top) in enumerate(zip(keyed, tops)):
                ranked[cg + off] = kd + top
        s1, s2, v1, i1, v2, i2, cand, best, bpos = ranked.pop(cg)
        tau = best[k - 1:k, :]
        last = bpos[k - 1:k, :]
        pos = lax.broadcasted_iota(jnp.int32, cand.shape, 0).astype(F32)
        picked = jnp.where(cand > tau, 1.0,
                           jnp.where(cand == tau, jnp.where(pos <= last, 1.0, 0.0), 0.0))
        nb = [jnp.sum(picked[0:k, :], axis=0, keepdims=True)]
        nb += [jnp.sum(picked[k + (a - 1) * hk:k + a * hk, :], axis=0, keepdims=True)
               for a in range(1, hk)]
        tail = k + (hk - 1) * hk
        nb += [picked[tail + a:tail + a + 1, :] for a in range(k - hk)]
        rows = lax.broadcasted_iota(jnp.int32, (n_keys, LANES), 0).astype(F32)
        nb_dense = jnp.zeros((n_keys, LANES), F32)
        rb = jnp.full((n_keys, LANES), RANK_NONE, F32)
        for r in range(k):
            nb_dense = jnp.where(rows == i1[r:r + 1, :], nb[r], nb_dense)
            rb = jnp.where(rows == i2[r:r + 1, :], float(r), rb)
        z = jnp.sum(jnp.exp(best - best[0:1, :]), axis=0, keepdims=True)
        a_ref[:, lanes] = jnp.exp(s1 - v1[0:1, :])
        b_ref[:, lanes] = jnp.exp(s2 - v2[0:1, :]) / z
        nb_ref[:, lanes] = nb_dense
        rb_ref[:, lanes] = rb


def _peer_select(q_t, sub_keys):
    n_heads, _, n_keys, half = sub_keys.shape
    s = q_t.shape[1]
    assert PEER_TOPK % (2 * SUBLANES) == 0 and n_keys >= PEER_TOPK
    tn = _tile(s, 512)
    dense = jax.ShapeDtypeStruct((n_heads, n_keys, s), F32)
    dense_spec = pl.BlockSpec((None, n_keys, tn), lambda i, h: (h, 0, i))
    nbytes = (_nbytes((2 * half, tn), BF16) + _nbytes((2, n_keys, half), BF16)
              + 4 * _nbytes((n_keys, tn), F32))
    return pl.pallas_call(
        functools.partial(_peer_select_kernel, half=half, tn=tn),
        grid=(s // tn, n_heads),
        in_specs=[pl.BlockSpec((2 * half, tn), lambda i, h: (h, i)),
                  pl.BlockSpec((None, 2, n_keys, half), lambda i, h: (h, 0, 0, 0))],
        out_specs=[dense_spec] * 4,
        out_shape=[dense] * 4,
        compiler_params=_params(("parallel", "parallel"), nbytes, 8 * 2**20),
        name="peer_select",
    )(q_t, sub_keys)


def _gelu(x):
    return 0.5 * x * (1.0 + lax.erf(x * (2.0 ** -0.5)))


def _peer_dense_kernel(ht_ref, dn_ref, upt_ref, a_ref, nb_ref, b_ref, rb_ref, o_ref,
                       pt0_ref, pt1_ref, wt0_ref, wt1_ref,
                       *, n_heads, n_keys, n_tiles, te, tn, irows, jc, pieces):
    t = pl.program_id(0)
    n_pairs = pl.num_programs(0) - 2
    rps = te // n_keys
    gate_tile = jnp.clip(t - 1, 0, n_pairs - 1) % n_tiles
    ioff = (gate_tile * rps) % irows
    chunks_per_i = n_keys // jc

    @pl.when(t == 0)
    def _():
        pt1_ref[...] = jnp.zeros_like(pt1_ref)
        wt0_ref[...] = jnp.zeros_like(wt0_ref)

    @pl.when(jnp.logical_or(t == 0, (t - 2) % n_tiles == 0))
    def _():
        o_ref[...] = jnp.zeros_like(o_ref)

    def step(pt_new, pt_old, wt_new, wt_old):
        def gate_chunk(c):
            irow = ioff + c // chunks_per_i
            jrows = pl.ds((c % chunks_per_i) * jc, jc)
            gate = jnp.zeros((jc, tn), F32)
            for h in range(n_heads):
                prod = a_ref[h, pl.ds(irow, 1), :] * b_ref[h, jrows, :]
                picked = rb_ref[h, jrows, :] < nb_ref[h, pl.ds(irow, 1), :]
                gate = gate + jnp.where(picked, prod, 0.0)
            act = _gelu(pt_old[pl.ds(c * jc, jc), :])
            wt_new[pl.ds(c * jc, jc), :] = (gate * act).astype(wt_new.dtype)

        n_chunks = te // jc
        d_per = ht_ref.shape[0] // pieces
        for p in range(pieces):
            span = pl.ds(p * d_per, d_per)
            o_ref[span, :] += jnp.dot(upt_ref[span, :], wt_old[...], preferred_element_type=F32)
            for c in range(2 * p * n_chunks // (2 * pieces), (2 * p + 1) * n_chunks // (2 * pieces)):
                gate_chunk(c)
            part = jnp.dot(dn_ref[:, span], ht_ref[span, :], preferred_element_type=F32)
            if p == 0:
                pt_new[...] = part
            else:
                pt_new[...] += part
            for c in range((2 * p + 1) * n_chunks // (2 * pieces), (2 * p + 2) * n_chunks // (2 * pieces)):
                gate_chunk(c)

    @pl.when(t % 2 == 0)
    def _():
        step(pt0_ref, pt1_ref, wt1_ref, wt0_ref)

    @pl.when(t % 2 == 1)
    def _():
        step(pt1_ref, pt0_ref, wt0_ref, wt1_ref)


def _peer_dense(h_t, down, up, sel):
    a, nb, b, rb = sel
    n_heads, n_keys, s = a.shape
    d = h_t.shape[0]
    n_exp = down.shape[0]
    tn = _tile(s, 512)
    te = _tile(n_exp, 512)
    assert te % n_keys == 0
    n_tiles = n_exp // te
    rps = te // n_keys
    irows = max(rps, SUBLANES)
    assert irows % rps == 0 and n_keys % irows == 0
    jc = 16
    pieces = next(p for p in (16, 8, 4, 2, 1)
                  if d % (p * LANES) == 0 and (te // jc) % (2 * p) == 0)
    up_t = jnp.transpose(up.reshape(n_tiles, te, d), (0, 2, 1)).astype(BF16)
    n_pairs = (s // tn) * n_tiles
    pair = lambda t, lag: jnp.clip(t - lag, 0, n_pairs - 1)
    tok = lambda t, lag: pair(t, lag) // n_tiles
    exp = lambda t, lag: pair(t, lag) % n_tiles
    once = pl.Buffered(1)
    by_i = pl.BlockSpec((n_heads, irows, tn), lambda t: (0, (exp(t, 1) * rps) // irows, tok(t, 1)))
    by_j = pl.BlockSpec((n_heads, n_keys, tn), lambda t: (0, 0, tok(t, 1)), pipeline_mode=once)
    nbytes = (2 * _nbytes((te, d), BF16) + 2 * _nbytes((n_heads, irows, tn), F32)
              + _nbytes((d, tn), F32) + _nbytes((d, tn), BF16))
    scratch = (2 * _nbytes((te, tn), F32) + 2 * _nbytes((te, tn), BF16)
               + 2 * _nbytes((n_heads, n_keys, tn), F32))
    return pl.pallas_call(
        functools.partial(_peer_dense_kernel, n_heads=n_heads, n_keys=n_keys, n_tiles=n_tiles,
                          te=te, tn=tn, irows=irows, jc=jc, pieces=pieces),
        grid=(n_pairs + 2,),
        in_specs=[pl.BlockSpec((d, tn), lambda t: (0, tok(t, 0))),
                  pl.BlockSpec((te, d), lambda t: (exp(t, 0), 0)),
                  pl.BlockSpec((None, d, te), lambda t: (exp(t, 2), 0, 0)),
                  by_i, by_i, by_j, by_j],
        out_specs=pl.BlockSpec((d, tn), lambda t: (0, tok(t, 2))),
        out_shape=jax.ShapeDtypeStruct((d, s), F32),
        scratch_shapes=[pltpu.VMEM((te, tn), F32), pltpu.VMEM((te, tn), F32),
                        pltpu.VMEM((te, tn), BF16), pltpu.VMEM((te, tn), BF16)],
        compiler_params=_params(("arbitrary",), nbytes, scratch),
        name="peer_dense",
    )(h_t, down, up_t, a, nb, b, rb)


def _add_t_kernel(x_ref, yt_ref, g_ref, o_ref, *, normalize):
    x = x_ref[...] + yt_ref[...].T
    if normalize:
        x = _rmsnorm_rows(x, g_ref[...])
    o_ref[...] = x


def _add_transposed(x, y_t, g, *, normalize):
    s, d = x.shape
    bm = _tile(s, 256)
    nbytes = 3 * _nbytes((bm, d), F32)
    return pl.pallas_call(
        functools.partial(_add_t_kernel, normalize=normalize),
        grid=(s // bm,),
        in_specs=[pl.BlockSpec((bm, d), lambda i: (i, 0)),
                  pl.BlockSpec((d, bm), lambda i: (0, i)),
                  pl.BlockSpec((1, d), lambda i: (0, 0))],
        out_specs=pl.BlockSpec((bm, d), lambda i: (i, 0)),
        out_shape=jax.ShapeDtypeStruct((s, d), F32),
        compiler_params=_params(("parallel",), nbytes, 2 * _nbytes((bm, d), F32)),
        name="add_transposed",
    )(x, y_t, g.reshape(1, d))


def _layer(x, mem, p, *, last, norm_final):
    d = x.shape[1]
    sb_w = p["w_branch_sb"].shape[0]
    pool_w = p["w_branch_pool"].shape[0]
    n_sb_heads = sb_w // HEAD_DIM
    xa_w = p["xa_w_q"].shape[1]
    bf = lambda w: w.astype(BF16)

    w_in = bf(p["w_in"])
    h1 = _rmsnorm(x, p["norm_mix"], out_dtype=BF16)
    qkv = _matmul(h1, w_in, out_dtype=BF16, n_cols=3 * sb_w)
    u_pool = _matmul(h1, w_in, out_dtype=F32, col_start=3 * sb_w, n_cols=pool_w)
    gates = _matmul(h1, w_in, out_dtype=F32, epilogue="sigmoid", col_start=3 * sb_w + pool_w)
    attn = _sb_attention(qkv, n_heads=n_sb_heads)
    pooled = _multiscale_pool(u_pool, bf(p["pool_group_w"]), p["pool_scale"])
    mix = _gated_mix(attn, pooled, p["w_branch_sb"], p["w_branch_pool"], gates, d_model=d)
    x1 = _matmul(mix, bf(p["w_out"]), out_dtype=F32, epilogue="residual", residual=x)

    mem_n = _rmsnorm(mem, p["norm_mem_kv"], out_dtype=BF16)
    xkv = _matmul(mem_n, bf(p["xa_w_kv"]), out_dtype=BF16)
    x2, h3_t = _memory_block(x1, xkv, p["norm_mem_q"], bf(p["xa_w_q"]), bf(p["xa_w_o"]),
                             p["norm_ffn"], n_heads=xa_w // HEAD_DIM)

    q_t = _matmul(bf(p["peer_w_query"].T), h3_t, out_dtype=BF16)
    sel = _peer_select(q_t, bf(p["peer_sub_keys"]))
    y_t = _peer_dense(h3_t, bf(p["peer_down"]), p["peer_up"], sel)
    return _add_transposed(x2, y_t, norm_final, normalize=last)


_LAYER_PARAMS = ("norm_mix", "norm_mem_q", "norm_mem_kv", "norm_ffn", "w_in", "pool_group_w",
                 "pool_scale", "w_branch_sb", "w_branch_pool", "w_out", "xa_w_q", "xa_w_kv",
                 "xa_w_o", "peer_w_query", "peer_sub_keys", "peer_down", "peer_up")


def kernel(x, mem, norm_mix, norm_mem_q, norm_mem_kv, norm_ffn, norm_final, w_in, pool_group_w,
           pool_scale, w_branch_sb, w_branch_pool, w_out, xa_w_q, xa_w_kv, xa_w_o,
           peer_w_query, peer_sub_keys, peer_down, peer_up):
    stacked = dict(norm_mix=norm_mix, norm_mem_q=norm_mem_q, norm_mem_kv=norm_mem_kv,
                   norm_ffn=norm_ffn, w_in=w_in, pool_group_w=pool_group_w, pool_scale=pool_scale,
                   w_branch_sb=w_branch_sb, w_branch_pool=w_branch_pool, w_out=w_out,
                   xa_w_q=xa_w_q, xa_w_kv=xa_w_kv, xa_w_o=xa_w_o, peer_w_query=peer_w_query,
                   peer_sub_keys=peer_sub_keys, peer_down=peer_down, peer_up=peer_up)
    depth = w_in.shape[0]
    outs = []
    for b in range(x.shape[0]):
        xb = x[b]
        for l in range(depth):
            p = {name: stacked[name][l] for name in _LAYER_PARAMS}
            xb = _layer(xb, mem[b], p, last=(l == depth - 1), norm_final=norm_final)
        outs.append(xb)
    return jnp.stack(outs, axis=0)
```
